```python
import math
import jax, jax.numpy as jnp
from jax import lax
import numpy as np

D_MODEL = 1024
BATCH = 2
SEQ = 16384
DEPTH = 2

CHUNK = 64
DN_HEADS = 8
DN_DK = 128
DN_DV = 128
DN_WIDTH = DN_HEADS * DN_DK
CONV_K = 4
SA_HEADS = 8
SA_HD = 128
SA_WIDTH = SA_HEADS * SA_HD
IDX_HEADS = 8
IDX_HD = 64
TOPK_MAX = 256
Q_BLOCK = 128
XA_HEADS = 4
XA_HD = 256
XA_WIDTH = XA_HEADS * XA_HD
N_MEM = 256
ROPE_THETA = 500000.0
ROPE_DIV = 4
D_FF = ((8 * D_MODEL // 3 + 255) // 256) * 256
N_BRANCH = 3
NORM_EPS = 1e-6

IN_SIZES = (DN_WIDTH, DN_WIDTH, DN_HEADS * DN_DV, DN_HEADS * DN_DV, DN_HEADS, DN_HEADS,
            SA_WIDTH, SA_WIDTH, SA_WIDTH, IDX_HEADS * IDX_HD, IDX_HD, IDX_HEADS,
            XA_WIDTH, N_BRANCH * D_MODEL)
N_IN = sum(IN_SIZES)

kernel_name = 'hybrid_gdn_dsa_memxattn_block'


def rmsnorm(x, g):
    xf = x.astype(jnp.float32)
    y = xf * lax.rsqrt(jnp.mean(xf * xf, axis=-1, keepdims=True) + NORM_EPS)
    return (y * g.astype(jnp.float32)).astype(x.dtype)


def l2norm(x):
    return x * lax.rsqrt(jnp.sum(x * x, axis=-1, keepdims=True) + NORM_EPS)


def rope_partial(x, positions):
    rd = x.shape[-1] // ROPE_DIV
    half = rd // 2
    inv_freq = ROPE_THETA ** (-(jnp.arange(half, dtype=jnp.float32) * 2.0 / rd))
    ang = positions.astype(jnp.float32)[..., None] * inv_freq
    cos = jnp.cos(ang)[:, :, None, :]
    sin = jnp.sin(ang)[:, :, None, :]
    xr = x[..., :rd].astype(jnp.float32)
    x1, x2 = xr[..., :half], xr[..., half:]
    rot = jnp.concatenate([x1 * cos - x2 * sin, x2 * cos + x1 * sin], axis=-1)
    return jnp.concatenate([rot.astype(x.dtype), x[..., rd:]], axis=-1)


def causal_conv(x, w):
    c = x.shape[-1]
    return lax.conv_general_dilated(x, w[:, None, :], window_strides=(1,),
                                    padding=[(CONV_K - 1, 0)],
                                    dimension_numbers=('NWC', 'WIO', 'NWC'),
                                    feature_group_count=c)


def gated_delta_rule(q, k, v, beta, g):
    b, s, h, dk = q.shape
    dv = v.shape[-1]
    n = s // CHUNK

    def chunks(t):
        return jnp.moveaxis(t.reshape((b, n, CHUNK) + t.shape[2:]), 2, 3)

    qc, kc, vc, bc, gc = (chunks(t) for t in (q, k, v, beta, g))
    gc = jnp.cumsum(gc, axis=-1)
    pos = jnp.arange(CHUNK)
    incl = pos[:, None] >= pos[None, :]
    strict = pos[:, None] > pos[None, :]
    diff = gc[..., :, None] - gc[..., None, :]
    decay = jnp.where(incl, jnp.exp(jnp.where(incl, diff, 0.0)), 0.0)
    kb = kc * bc[..., None]
    m = jnp.where(strict, jnp.einsum('bnhid,bnhjd->bnhij', kb, kc) * decay, 0.0)
    rhs = jnp.concatenate([vc * bc[..., None], kb * jnp.exp(gc)[..., None]], axis=-1)
    sol = lax.linalg.triangular_solve(m, rhs, left_side=True, lower=True, unit_diagonal=True)
    u, w = sol[..., :dv], sol[..., dv:]
    a_qk = jnp.einsum('bnhid,bnhjd->bnhij', qc, kc) * decay
    q_dec = qc * jnp.exp(gc)[..., None]
    g_last = gc[..., -1]
    k_dec = kc * jnp.exp(g_last[..., None] - gc)[..., None]
    c_dec = jnp.exp(g_last)

    def step(state, xs):
        qd, kd, a, uu, ww, cd = xs
        v_new = uu - jnp.einsum('bhck,bhkv->bhcv', ww, state)
        o = jnp.einsum('bhck,bhkv->bhcv', qd, state) + jnp.einsum('bhij,bhjv->bhiv', a, v_new)
        state = state * cd[..., None, None] + jnp.einsum('bhck,bhcv->bhkv', kd, v_new)
        return state, o

    xs = tuple(jnp.moveaxis(t, 1, 0) for t in (q_dec, k_dec, a_qk, u, w, c_dec))
    s0 = jnp.zeros((b, h, dk, dv), jnp.float32)
    _, o = lax.scan(step, s0, xs)
    return jnp.swapaxes(jnp.moveaxis(o, 0, 1), 2, 3).reshape(b, s, h, dv)


def deltanet_branch(q_in, k_in, v_in, z_in, b_in, a_in, conv_w, a_log, dt_bias, dn_norm):
    b, s, _ = q_in.shape
    f32 = jnp.float32
    qkv = jax.nn.silu(causal_conv(jnp.concatenate([q_in, k_in, v_in], axis=-1), conv_w))
    q, k, v = jnp.split(qkv, [DN_WIDTH, 2 * DN_WIDTH], axis=-1)
    q = l2norm(q.reshape(b, s, DN_HEADS, DN_DK).astype(f32)) * DN_DK ** -0.5
    k = l2norm(k.reshape(b, s, DN_HEADS, DN_DK).astype(f32))
    v = v.reshape(b, s, DN_HEADS, DN_DV).astype(f32)
    beta = jax.nn.sigmoid(b_in.astype(f32))
    g = -jnp.exp(a_log.astype(f32)) * jax.nn.softplus(a_in.astype(f32) + dt_bias.astype(f32))
    o = gated_delta_rule(q, k, v, beta, g)
    z = z_in.reshape(b, s, DN_HEADS, DN_DV).astype(f32)
    o = rmsnorm(o, dn_norm) * jax.nn.silu(z)
    return o.reshape(b, s, DN_WIDTH).astype(q_in.dtype)


def dsa_branch(q_in, k_in, v_in, iq_in, ik_in, iw_in, positions):
    b, s, _ = q_in.shape
    f32 = jnp.float32
    q = rope_partial(q_in.reshape(b, s, SA_HEADS, SA_HD), positions)
    k = rope_partial(k_in.reshape(b, s, SA_HEADS, SA_HD), positions)
    v = v_in.reshape(b, s, SA_HEADS, SA_HD)
    iq = rope_partial(iq_in.reshape(b, s, IDX_HEADS, IDX_HD), positions).astype(f32)
    ik = rope_partial(ik_in.reshape(b, s, 1, IDX_HD), positions)[:, :, 0].astype(f32)
    iw = iw_in.astype(f32) * (IDX_HEADS ** -0.5 * IDX_HD ** -0.5)
    k_sel = min(TOPK_MAX, s // 4)
    nb = s // Q_BLOCK
    key_chunk = jnp.arange(s) // CHUNK

    def blocks(t):
        return jnp.moveaxis(t.reshape((b, nb, Q_BLOCK) + t.shape[2:]), 1, 0)

    def attend(xs):
        qb, iqb, iwb, blk = xs
        q_chunk = (blk * Q_BLOCK + jnp.arange(Q_BLOCK)) // CHUNK
        score = jax.nn.relu(jnp.einsum('bthd,bsd->bths', iqb, ik))
        score = jnp.einsum('bths,bth->bts', score, iwb)
        admissible = key_chunk[None, :] <= q_chunk[:, None]
        score = jnp.where(admissible[None], score, -jnp.inf)
        _, sel = lax.top_k(score, k_sel)
        valid = (sel // CHUNK) <= q_chunk[None, :, None]
        kg = jax.vmap(lambda a, i: a[i])(k, sel)
        vg = jax.vmap(lambda a, i: a[i])(v, sel)
        logits = jnp.einsum('bthd,btkhd->bthk', qb, kg).astype(f32) * SA_HD ** -0.5
        logits = jnp.where(valid[:, :, None, :], logits, -jnp.inf)
        p = jax.nn.softmax(logits, axis=-1).astype(vg.dtype)
        return jnp.einsum('bthk,btkhd->bthd', p, vg)

    o = lax.map(attend, (blocks(q), blocks(iq), blocks(iw), jnp.arange(nb, dtype=jnp.int32)))
    return jnp.moveaxis(o, 0, 1).reshape(b, s, SA_WIDTH)


def memory_branch(q_in, mem_n, w_mem_kv):
    b, s, _ = q_in.shape
    m = mem_n.shape[1]
    mk, mv = jnp.split(mem_n @ w_mem_kv, 2, axis=-1)
    q = q_in.reshape(b, s, XA_HEADS, XA_HD)
    mk = mk.reshape(b, m, XA_HEADS, XA_HD)
    mv = mv.reshape(b, m, XA_HEADS, XA_HD)
    logits = jnp.einsum('bshd,bmhd->bhsm', q, mk).astype(jnp.float32) * XA_HD ** -0.5
    p = jax.nn.softmax(logits, axis=-1).astype(mv.dtype)
    return jnp.einsum('bhsm,bmhd->bshd', p, mv).reshape(b, s, XA_WIDTH)


def setup_inputs(seed: int = 0) -> dict:
    key = jax.random.key(seed)
    ks = jax.random.split(key, 20)
    f32 = jnp.float32

    def nrm(k, shape, fan_in):
        return jax.random.normal(k, shape, f32) * fan_in ** -0.5

    def gain(k, shape):
        return 1.0 + 0.02 * jax.random.normal(k, shape, f32)

    x = jax.random.normal(ks[0], (BATCH, SEQ, D_MODEL), f32)
    mem = jax.random.normal(ks[1], (BATCH, N_MEM, D_MODEL), f32)
    offset = jax.random.randint(ks[2], (BATCH, 1), 0, 64, dtype=jnp.int32) * CHUNK
    positions = (offset + jnp.arange(SEQ, dtype=jnp.int32)[None, :]).astype(jnp.int32)
    dt = jnp.exp(jax.random.uniform(ks[3], (DEPTH, DN_HEADS), f32, math.log(1e-3), math.log(1e-1)))
    return {
        'x': x,
        'mem': mem,
        'positions': positions,
        'norm_mix': gain(ks[4], (DEPTH, D_MODEL)),
        'norm_mem': gain(ks[5], (DEPTH, D_MODEL)),
        'w_in': nrm(ks[6], (DEPTH, D_MODEL, N_IN), D_MODEL),
        'b_gate': 0.1 * jax.random.normal(ks[7], (DEPTH, N_BRANCH, D_MODEL), f32),
        'conv_w': nrm(ks[8], (DEPTH, CONV_K, 3 * DN_WIDTH), CONV_K),
        'a_log': jnp.log(jax.random.uniform(ks[9], (DEPTH, DN_HEADS), f32, 1.0, 16.0)),
        'dt_bias': jnp.log(jnp.expm1(dt)),
        'dn_norm': gain(ks[10], (DEPTH, DN_DV)),
        'w_mem_kv': nrm(ks[11], (DEPTH, D_MODEL, 2 * XA_WIDTH), D_MODEL),
        'w_branch': nrm(ks[12], (DEPTH, N_BRANCH, DN_WIDTH, D_MODEL), DN_WIDTH),
        'w_out': nrm(ks[13], (DEPTH, D_MODEL, D_MODEL), D_MODEL),
        'norm_ffn': gain(ks[14], (DEPTH, D_MODEL)),
        'w_ffn_in': nrm(ks[15], (DEPTH, D_MODEL, 2 * D_FF), D_MODEL),
        'w_ffn_out': nrm(ks[16], (DEPTH, D_FF, D_MODEL), D_FF),
        'norm_final': gain(ks[17], (D_MODEL,)),
    }


def reference(x, mem, positions, norm_mix, norm_mem, w_in, b_gate, conv_w, a_log, dt_bias,
              dn_norm, w_mem_kv, w_branch, w_out, norm_ffn, w_ffn_in, w_ffn_out, norm_final):
    b, s, d = x.shape
    split_pts = []
    acc = 0
    for size in IN_SIZES[:-1]:
        acc += size
        split_pts.append(acc)
    for l in range(DEPTH):
        h = rmsnorm(x, norm_mix[l])
        (dq, dk, dv, dz, db, da, sq, sk, sv, iq, ik, iw, xq, gl) = jnp.split(h @ w_in[l], split_pts, axis=-1)
        o_dn = deltanet_branch(dq, dk, dv, dz, db, da, conv_w[l], a_log[l], dt_bias[l], dn_norm[l])
        o_sa = dsa_branch(sq, sk, sv, iq, ik, iw, positions)
        o_xa = memory_branch(xq, rmsnorm(mem, norm_mem[l]), w_mem_kv[l])
        gates = jax.nn.sigmoid((gl.reshape(b, s, N_BRANCH, d) + b_gate[l]).astype(jnp.float32)).astype(x.dtype)
        merged = (gates[:, :, 0] * (o_dn @ w_branch[l, 0])
                  + gates[:, :, 1] * (o_sa @ w_branch[l, 1])
                  + gates[:, :, 2] * (o_xa @ w_branch[l, 2]))
        x = x + merged @ w_out[l]
        h = rmsnorm(x, norm_ffn[l])
        gt, up = jnp.split(h @ w_ffn_in[l], 2, axis=-1)
        x = x + (jax.nn.silu(gt) * up) @ w_ffn_out[l]
    return rmsnorm(x, norm_final)
```

```python
import functools

import jax
import jax.numpy as jnp
import numpy as np
from jax import lax
from jax.experimental import pallas as pl
from jax.experimental.pallas import tpu as pltpu

F32 = jnp.float32
BF16 = jnp.bfloat16

D_MODEL = 1024
CHUNK = 64
DN_HEADS = 8
DN_DK = 128
DN_WIDTH = DN_HEADS * DN_DK
CONV_K = 4
SA_HEADS = 8
SA_HD = 128
SA_WIDTH = SA_HEADS * SA_HD
IDX_HEADS = 8
IDX_HD = 64
TOPK_MAX = 256
XA_HEADS = 4
XA_HD = 256
XA_WIDTH = XA_HEADS * XA_HD
ROPE_THETA = 500000.0
ROPE_DIV = 4
D_FF = 2816
N_BRANCH = 3
NORM_EPS = 1e-6

LANES = 128
HALO = 8
VMEM_LIMIT = 48 * 1024 * 1024

OFF_DQKV = 0
OFF_DZ = 3 * DN_WIDTH
OFF_DB = OFF_DZ + DN_WIDTH
OFF_DA = OFF_DB + DN_HEADS
OFF_SQK = OFF_DA + DN_HEADS
OFF_SV = OFF_SQK + 2 * SA_WIDTH
OFF_IQ = OFF_SV + SA_WIDTH
OFF_IK = OFF_IQ + IDX_HEADS * IDX_HD
OFF_IW = OFF_IK + IDX_HD
OFF_XQ = OFF_IW + IDX_HEADS
OFF_GL = OFF_XQ + XA_WIDTH
N_IN = OFF_GL + N_BRANCH * D_MODEL

SM_DB = 0
SM_DA = DN_HEADS
SM_IW = 2 * DN_HEADS

NEG_BIAS = -1e30
INT_MIN = -2 ** 31
NEG_INF_KEY = -2 ** 31 + 0x7FFFFF


def _cparams(sem):
    return pltpu.CompilerParams(dimension_semantics=sem, vmem_limit_bytes=VMEM_LIMIT)


def _rms(x, g):
    return x * lax.rsqrt(jnp.mean(x * x, axis=-1, keepdims=True) + NORM_EPS) * g


def _sigmoid(x):
    return 1.0 / (1.0 + jnp.exp(-x))


def _dot(a, b):
    return jnp.dot(a, b, preferred_element_type=F32)


def _dot_nt(a, b):
    return lax.dot_general(a, b, (((1,), (1,)), ((), ())), preferred_element_type=F32)


def _dot_tn(a, b):
    return lax.dot_general(a, b, (((0,), (0,)), ((), ())), preferred_element_type=F32)


def _dot_f32(a, b):
    return jnp.dot(a, b, preferred_element_type=F32, precision=lax.Precision.HIGHEST)


def _nmm_kernel(n, x_ref, g_ref, *refs):
    h = _rms(x_ref[...], g_ref[...]).astype(BF16)
    for w_ref, o_ref in zip(refs[:n], refs[n:]):
        o_ref[...] = _dot(h, w_ref[...]).astype(o_ref.dtype)


def _norm_matmul(x2d, g, weights, dtypes, tm, name):
    rows, d = x2d.shape
    tm = min(tm, rows)
    n = len(weights)
    in_specs = [pl.BlockSpec((tm, d), lambda i: (i, 0)), pl.BlockSpec((1, d), lambda i: (0, 0))]
    in_specs += [pl.BlockSpec(w.shape, lambda i: (0, 0)) for w in weights]
    out_specs = [pl.BlockSpec((tm, w.shape[1]), lambda i: (i, 0)) for w in weights]
    out_shape = [jax.ShapeDtypeStruct((rows, w.shape[1]), dt) for w, dt in zip(weights, dtypes)]
    return pl.pallas_call(
        functools.partial(_nmm_kernel, n),
        out_shape=out_shape, grid=(rows // tm,), in_specs=in_specs, out_specs=out_specs,
        compiler_params=_cparams(("parallel",)), name=name,
    )(x2d, g.reshape(1, d), *weights)


def _dnprep_kernel(x_ref, halo_ref, cw_ref, sm_ref, al_ref, dt_ref, q_ref, k_ref, v_ref, gb_ref):
    i = pl.program_id(1)
    t = x_ref.shape[1]
    keep = (i > 0).astype(F32)
    for s, o_ref in enumerate((q_ref, k_ref, v_ref)):
        cols = slice(s * DN_WIDTH, (s + 1) * DN_WIDTH)
        xc = jnp.concatenate([halo_ref[0, :, cols] * keep, x_ref[0, :, cols]], axis=0)
        y = jnp.zeros((t, DN_WIDTH), F32)
        for j in range(CONV_K):
            lo = HALO - (CONV_K - 1) + j
            y = y + xc[lo:lo + t] * cw_ref[j:j + 1, cols]
        y = y * _sigmoid(y)
        if s == 2:
            o_ref[0] = y
        else:
            for h in range(DN_HEADS):
                hs = slice(h * DN_DK, (h + 1) * DN_DK)
                seg = y[:, hs]
                seg = seg * lax.rsqrt(jnp.sum(seg * seg, axis=-1, keepdims=True) + NORM_EPS)
                if s == 0:
                    seg = seg * DN_DK ** -0.5
                o_ref[0, :, hs] = seg
    sm = sm_ref[0]
    beta = _sigmoid(sm)
    z = sm + dt_ref[...]
    softplus = jnp.maximum(z, 0.0) + jnp.log1p(jnp.exp(-jnp.abs(z)))
    g = -jnp.exp(al_ref[...]) * softplus
    lane = lax.broadcasted_iota(jnp.int32, sm.shape, 1)
    gb_ref[0] = jnp.where(lane < SM_DA, beta, g)


def _dn_prep(qkv, small, conv_w, a_log, dt_bias, t):
    b, s, c = qkv.shape
    t = min(t, s)
    al = jnp.zeros((1, LANES), F32).at[0, SM_DA:SM_DA + DN_HEADS].set(a_log)
    dt = jnp.zeros((1, LANES), F32).at[0, SM_DA:SM_DA + DN_HEADS].set(dt_bias)
    hb = t // HALO
    out_sd = jax.ShapeDtypeStruct((b, s, DN_WIDTH), F32)
    o_spec = pl.BlockSpec((1, t, DN_WIDTH), lambda bi, i: (bi, i, 0))
    return pl.pallas_call(
        _dnprep_kernel,
        out_shape=[out_sd, out_sd, out_sd, jax.ShapeDtypeStruct((b, s, LANES), F32)],
        grid=(b, s // t),
        in_specs=[
            pl.BlockSpec((1, t, c), lambda bi, i: (bi, i, 0)),
            pl.BlockSpec((1, HALO, c), lambda bi, i: (bi, jnp.maximum(i * hb - 1, 0), 0)),
            pl.BlockSpec((CONV_K, c), lambda bi, i: (0, 0)),
            pl.BlockSpec((1, t, LANES), lambda bi, i: (bi, i, 0)),
            pl.BlockSpec((1, LANES), lambda bi, i: (0, 0)),
            pl.BlockSpec((1, LANES), lambda bi, i: (0, 0)),
        ],
        out_specs=[o_spec, o_spec, o_spec, pl.BlockSpec((1, t, LANES), lambda bi, i: (bi, i, 0))],
        compiler_params=_cparams(("parallel", "parallel")), name="dn_prep",
    )(qkv, qkv, conv_w, small, al, dt)


def _dn_kernel(q_ref, k_ref, v_ref, z_ref, gb_ref, nrm_ref, o_ref, s_scr):
    @pl.when(pl.program_id(1) == 0)
    def _():
        s_scr[...] = jnp.zeros_like(s_scr)

    c = CHUNK
    row = lax.broadcasted_iota(jnp.int32, (c, c), 0)
    col = lax.broadcasted_iota(jnp.int32, (c, c), 1)
    incl = row >= col
    strict = row > col
    tri = incl.astype(F32)
    eye = (row == col).astype(F32)
    gb = gb_ref[0]
    gc_all = _dot_f32(tri, gb)
    gc_t = gc_all.T
    for h in range(DN_HEADS):
        hs = slice(h * DN_DK, (h + 1) * DN_DK)
        qh = q_ref[0, :, hs]
        kh = k_ref[0, :, hs]
        vh = v_ref[0, :, hs]
        beta = gb[:, SM_DB + h:SM_DB + h + 1]
        gcol = gc_all[:, SM_DA + h:SM_DA + h + 1]
        grow = gc_t[SM_DA + h:SM_DA + h + 1, :]
        decay = jnp.where(incl, jnp.exp(jnp.where(incl, gcol - grow, 0.0)), 0.0)
        kb = kh * beta
        khb = kh.astype(BF16)
        m = jnp.where(strict, _dot_nt(kb.astype(BF16), khb) * decay, 0.0)
        x = -m
        p = eye + x
        for _ in range(5):
            x = _dot_f32(x, x)
            p = p + _dot_f32(p, x)
        eg = jnp.exp(gcol)
        rhs = jnp.concatenate([vh * beta, kb * eg], axis=1)
        sol = _dot_f32(p, rhs)
        u = sol[:, :DN_DK]
        w = sol[:, DN_DK:]
        aqk = _dot_nt(qh.astype(BF16), khb) * decay
        qd = qh * eg
        glast = gc_all[c - 1:c, SM_DA + h:SM_DA + h + 1]
        kd = kh * jnp.exp(glast - gcol)
        cd = jnp.exp(glast)
        st = s_scr[h]
        stb = st.astype(BF16)
        vnew = u - _dot(w.astype(BF16), stb)
        vnb = vnew.astype(BF16)
        o = _dot(qd.astype(BF16), stb) + _dot(aqk.astype(BF16), vnb)
        s_scr[h] = st * cd + _dot_tn(kd.astype(BF16), vnb)
        o = _rms(o, nrm_ref[...])
        zh = z_ref[0, :, hs]
        o_ref[0, :, hs] = (o * (zh * _sigmoid(zh))).astype(o_ref.dtype)


def _dn_main(q, k, v, z, gb, dn_norm):
    b, s, _ = q.shape
    blk = pl.BlockSpec((1, CHUNK, DN_WIDTH), lambda bi, i: (bi, i, 0))
    return pl.pallas_call(
        _dn_kernel,
        out_shape=jax.ShapeDtypeStruct((b, s, DN_WIDTH), BF16),
        grid=(b, s // CHUNK),
        in_specs=[blk, blk, blk, blk,
                  pl.BlockSpec((1, CHUNK, LANES), lambda bi, i: (bi, i, 0)),
                  pl.BlockSpec((1, DN_DK), lambda bi, i: (0, 0))],
        out_specs=blk,
        scratch_shapes=[pltpu.VMEM((DN_HEADS, DN_DK, DN_DK), F32)],
        compiler_params=_cparams(("parallel", "arbitrary")), name="dn_main",
    )(q, k, v, z, gb, dn_norm.reshape(1, DN_DK))


def _rope_tab_kernel(pos_ref, f_ref, sg_ref, cs_ref, sn_ref):
    ang = pos_ref[0].astype(F32) * f_ref[...]
    cs_ref[0] = jnp.cos(ang)
    sn_ref[0] = jnp.sin(ang) * sg_ref[...]


def _lane_tables(head_dim):
    rd = head_dim // ROPE_DIV
    half = rd // 2
    inv_freq = ROPE_THETA ** (-(jnp.arange(half, dtype=F32) * 2.0 / rd))
    lane = np.arange(LANES) % head_dim
    sel = np.where(lane < rd, lane % half, 0)
    freq = jnp.where(jnp.asarray(lane < rd), inv_freq[sel], 0.0).reshape(1, LANES)
    sign = np.where(lane < half, -1.0, np.where(lane < rd, 1.0, 0.0)).astype(np.float32)
    return freq.astype(F32), jnp.asarray(sign).reshape(1, LANES)


def _rope_tables(positions, head_dim, t):
    b, s = positions.shape
    t = min(t, s)
    freq, sign = _lane_tables(head_dim)
    sd = jax.ShapeDtypeStruct((b, s, LANES), F32)
    blk = pl.BlockSpec((1, t, LANES), lambda bi, i: (bi, i, 0))
    vec = pl.BlockSpec((1, LANES), lambda bi, i: (0, 0))
    return pl.pallas_call(
        _rope_tab_kernel, out_shape=[sd, sd], grid=(b, s // t),
        in_specs=[pl.BlockSpec((1, t, 1), lambda bi, i: (bi, i, 0)), vec, vec],
        out_specs=[blk, blk],
        compiler_params=_cparams(("parallel", "parallel")), name="rope_tables",
    )(positions.reshape(b, s, 1), freq, sign)


def _rope(x, cs, sn, first_half, half):
    partner = jnp.where(first_half, pltpu.roll(x, LANES - half, 1), pltpu.roll(x, half, 1))
    return x * cs + partner * sn


def _saprep_kernel(qk_ref, ix_ref, sm_ref, cs_ref, sn_ref, ci_ref, si_ref,
                   q_ref, k_ref, iq_ref, ik_ref, iw_ref):
    t = qk_ref.shape[1]
    lane = lax.broadcasted_iota(jnp.int32, (t, LANES), 1)
    half_sa = SA_HD // ROPE_DIV // 2
    half_ix = IDX_HD // ROPE_DIV // 2
    fh_sa = lane < half_sa
    fh_ix = (lane % IDX_HD) < half_ix
    cs, sn = cs_ref[0], sn_ref[0]
    ci, si = ci_ref[0], si_ref[0]
    for h in range(SA_HEADS):
        hs = slice(h * SA_HD, (h + 1) * SA_HD)
        q_ref[0, :, hs] = (_rope(qk_ref[0, :, hs], cs, sn, fh_sa, half_sa)
                           * SA_HD ** -0.5).astype(BF16)
        ks = slice(SA_WIDTH + h * SA_HD, SA_WIDTH + (h + 1) * SA_HD)
        k_ref[0, :, hs] = _rope(qk_ref[0, :, ks], cs, sn, fh_sa, half_sa).astype(BF16)
    for j in range(IDX_HEADS * IDX_HD // LANES):
        r = _rope(ix_ref[0, :, j * LANES:(j + 1) * LANES], ci, si, fh_ix, half_ix).astype(BF16)
        iq_ref[0, 2 * j] = r[:, :IDX_HD]
        iq_ref[0, 2 * j + 1] = r[:, IDX_HD:]
    nq = IDX_HEADS * IDX_HD
    ik_ref[0] = _rope(ix_ref[0, :, nq:nq + LANES], ci, si, fh_ix, half_ix).astype(BF16)
    iw_ref[0] = sm_ref[0] * (IDX_HEADS ** -0.5 * IDX_HD ** -0.5)


def _sa_prep(sqk, idx, small, tabs_sa, tabs_ix, t):
    b, s, _ = sqk.shape
    t = min(t, s)
    tab = pl.BlockSpec((1, t, LANES), lambda bi, i: (bi, i, 0))
    wide = pl.BlockSpec((1, t, SA_WIDTH), lambda bi, i: (bi, i, 0))
    return pl.pallas_call(
        _saprep_kernel,
        out_shape=[jax.ShapeDtypeStruct((b, s, SA_WIDTH), BF16),
                   jax.ShapeDtypeStruct((b, s, SA_WIDTH), BF16),
                   jax.ShapeDtypeStruct((b, IDX_HEADS, s, IDX_HD), BF16),
                   jax.ShapeDtypeStruct((b, s, LANES), BF16),
                   jax.ShapeDtypeStruct((b, s, LANES), F32)],
        grid=(b, s // t),
        in_specs=[pl.BlockSpec((1, t, 2 * SA_WIDTH), lambda bi, i: (bi, i, 0)),
                  pl.BlockSpec((1, t, idx.shape[2]), lambda bi, i: (bi, i, 0)),
                  tab, tab, tab, tab, tab],
        out_specs=[wide, wide,
                   pl.BlockSpec((1, IDX_HEADS, t, IDX_HD), lambda bi, i: (bi, 0, i, 0)),
                   tab, tab],
        compiler_params=_cparams(("parallel", "parallel")), name="sa_prep",
    )(sqk, idx, small, *tabs_sa, *tabs_ix)


def _select_kernel(iq_ref, ikt_ref, iw_ref, bias_ref, key_scr, *, tq, ksel, seq):
    i = pl.program_id(1)
    nkt = i + 1
    iw = iw_ref[0]
    q_chunk = (i * tq + lax.broadcasted_iota(jnp.int32, (tq, 1), 0)) // CHUNK
    col0 = lax.broadcasted_iota(jnp.int32, (1, tq), 1)

    def score_tile(kt, carry):
        off = pl.multiple_of(kt * tq, tq)
        ikt = ikt_ref[0, :, pl.ds(off, tq)]
        acc = jnp.zeros((tq, tq), F32)
        for h in range(IDX_HEADS):
            d = _dot(iq_ref[0, h], ikt)
            acc = acc + jnp.maximum(d, 0.0) * iw[:, SM_IW + h:SM_IW + h + 1]
        adm = ((off + col0) // CHUNK) <= q_chunk
        sc = jnp.where(adm, acc, -jnp.inf)
        sc = jnp.where(sc == 0.0, 0.0, sc)
        bits = pltpu.bitcast(sc, jnp.int32)
        key_scr[:, pl.ds(off, tq)] = bits ^ ((bits >> 31) & 0x7FFFFFFF)
        return carry

    lax.fori_loop(0, nkt, score_tile, 0)

    def count(pred):
        def body(kt, c):
            off = pl.multiple_of(kt * tq, tq)
            hit = jnp.where(pred(key_scr[:, pl.ds(off, tq)], off), 1.0, 0.0)
            for j in range(tq // LANES):
                c = c + hit[:, j * LANES:(j + 1) * LANES]
            return c
        c = lax.fori_loop(0, nkt, body, jnp.zeros((tq, LANES), F32))
        return jnp.sum(c, axis=1, keepdims=True)

    def count_ge(cand):
        return count(lambda keys, off: keys >= cand)

    kf = float(ksel)
    t0 = jnp.where(count_ge(jnp.zeros((tq, 1), jnp.int32)) >= kf, 0, INT_MIN).astype(jnp.int32)

    def bit_step(b, t):
        cand = t | jnp.left_shift(jnp.int32(1), 30 - b)
        return jnp.where(count_ge(cand) >= kf, cand, t)

    thr = lax.fori_loop(0, 31, bit_step, t0)
    finite = thr > NEG_INF_KEY
    thr_eff = jnp.maximum(thr, NEG_INF_KEY + 1)
    n_ge = count_ge(thr_eff)
    tie_rows = jnp.where(finite & (n_ge > kf), 1.0, 0.0)
    has_tie = jnp.max(tie_rows) > 0.0

    bias_ref[...] = jnp.full(bias_ref.shape, NEG_BIAS, bias_ref.dtype)

    @pl.when(jnp.logical_not(has_tie))
    def _():
        def write(kt, carry):
            off = pl.multiple_of(kt * tq, tq)
            sel = key_scr[:, pl.ds(off, tq)] >= thr_eff
            bias_ref[0, :, pl.ds(off, tq)] = jnp.where(sel, 0.0, NEG_BIAS).astype(bias_ref.dtype)
            return carry
        lax.fori_loop(0, nkt, write, 0)

    @pl.when(has_tie)
    def _():
        need = kf - count(lambda keys, off: keys > thr)
        nbits = max(1, int(np.ceil(np.log2(seq))))

        def idx_step(b, j):
            cand = j | jnp.left_shift(jnp.int32(1), nbits - 1 - b)
            below = count(lambda keys, off: (keys == thr) & ((off + col0) < cand))
            return jnp.where(below < need, cand, j)

        cut = lax.fori_loop(0, nbits, idx_step, jnp.zeros((tq, 1), jnp.int32))

        def write(kt, carry):
            off = pl.multiple_of(kt * tq, tq)
            keys = key_scr[:, pl.ds(off, tq)]
            tie_ok = (keys == thr) & ((off + col0) <= cut)
            sel = (keys >= thr_eff) & ((keys > thr) | tie_ok | jnp.logical_not(finite))
            bias_ref[0, :, pl.ds(off, tq)] = jnp.where(sel, 0.0, NEG_BIAS).astype(bias_ref.dtype)
            return carry
        lax.fori_loop(0, nkt, write, 0)


def _dsa_select(iq, ikt, iw, tq, ksel):
    b, _, s, _ = iq.shape
    tq = min(tq, s)
    return pl.pallas_call(
        functools.partial(_select_kernel, tq=tq, ksel=ksel, seq=s),
        out_shape=jax.ShapeDtypeStruct((b, s, s), BF16),
        grid=(b, s // tq),
        in_specs=[pl.BlockSpec((1, IDX_HEADS, tq, IDX_HD), lambda bi, i: (bi, 0, i, 0)),
                  pl.BlockSpec((1, IDX_HD, s), lambda bi, i: (bi, 0, 0)),
                  pl.BlockSpec((1, tq, LANES), lambda bi, i: (bi, i, 0))],
        out_specs=pl.BlockSpec((1, tq, s), lambda bi, i: (bi, i, 0)),
        scratch_shapes=[pltpu.VMEM((tq, s), jnp.int32)],
        compiler_params=_cparams(("parallel", "parallel")), name="dsa_select",
    )(iq, ikt, iw)


def _attn_kernel(q_ref, k_ref, v_ref, b_ref, o_ref, m_scr, l_scr, acc_scr):
    i = pl.program_id(1)
    j = pl.program_id(2)

    @pl.when(j == 0)
    def _():
        m_scr[...] = jnp.full(m_scr.shape, NEG_BIAS, F32)
        l_scr[...] = jnp.zeros_like(l_scr)
        acc_scr[...] = jnp.zeros_like(acc_scr)

    @pl.when(j <= i)
    def _():
        bias = b_ref[0].astype(F32)
        for h in range(SA_HEADS):
            hs = slice(h * SA_HD, (h + 1) * SA_HD)
            s = _dot_nt(q_ref[0, :, hs], k_ref[0, :, hs]) + bias
            m_old = m_scr[h]
            m_new = jnp.maximum(m_old, jnp.max(s, axis=1, keepdims=True))
            alpha = jnp.exp(m_old - m_new)
            p = jnp.exp(s - m_new[:, :1])
            l_scr[h] = alpha * l_scr[h] + jnp.sum(p, axis=1, keepdims=True)
            acc_scr[:, hs] = acc_scr[:, hs] * alpha + _dot(p.astype(BF16), v_ref[0, :, hs])
            m_scr[h] = m_new

    @pl.when(j == i)
    def _():
        for h in range(SA_HEADS):
            hs = slice(h * SA_HD, (h + 1) * SA_HD)
            o_ref[0, :, hs] = (acc_scr[:, hs] / l_scr[h]).astype(o_ref.dtype)


def _dsa_attend(q, k, v, bias, t):
    b, s, w = q.shape
    t = min(t, s)
    n = s // t
    return pl.pallas_call(
        _attn_kernel,
        out_shape=jax.ShapeDtypeStruct((b, s, w), BF16),
        grid=(b, n, n),
        in_specs=[pl.BlockSpec((1, t, w), lambda bi, i, j: (bi, i, 0)),
                  pl.BlockSpec((1, t, w), lambda bi, i, j: (bi, jnp.minimum(j, i), 0)),
                  pl.BlockSpec((1, t, w), lambda bi, i, j: (bi, jnp.minimum(j, i), 0)),
                  pl.BlockSpec((1, t, t), lambda bi, i, j: (bi, i, jnp.minimum(j, i)))],
        out_specs=pl.BlockSpec((1, t, w), lambda bi, i, j: (bi, i, 0)),
        scratch_shapes=[pltpu.VMEM((SA_HEADS, t, LANES), F32),
                        pltpu.VMEM((SA_HEADS, t, LANES), F32),
                        pltpu.VMEM((t, w), F32)],
        compiler_params=_cparams(("parallel", "parallel", "arbitrary")), name="dsa_attend",
    )(q, k, v, bias)


def _xattn_kernel(q_ref, mk_ref, mv_ref, o_ref):
    for h in range(XA_HEADS):
        hs = slice(h * XA_HD, (h + 1) * XA_HD)
        s = _dot_nt(q_ref[0, :, hs], mk_ref[0, :, hs]) * XA_HD ** -0.5
        p = jnp.exp(s - jnp.max(s, axis=1, keepdims=True))
        o = _dot(p.astype(BF16), mv_ref[0, :, hs]) / jnp.sum(p, axis=1, keepdims=True)
        o_ref[0, :, hs] = o.astype(o_ref.dtype)


def _mem_attend(xq, mk, mv, t):
    b, s, w = xq.shape
    t = min(t, s)
    m = mk.shape[1]
    return pl.pallas_call(
        _xattn_kernel,
        out_shape=jax.ShapeDtypeStruct((b, s, w), BF16),
        grid=(b, s // t),
        in_specs=[pl.BlockSpec((1, t, w), lambda bi, i: (bi, i, 0)),
                  pl.BlockSpec((1, m, w), lambda bi, i: (bi, 0, 0)),
                  pl.BlockSpec((1, m, w), lambda bi, i: (bi, 0, 0))],
        out_specs=pl.BlockSpec((1, t, w), lambda bi, i: (bi, i, 0)),
        compiler_params=_cparams(("parallel", "parallel")), name="mem_attend",
    )(xq, mk, mv)


def _merge_kernel(x_ref, od_ref, os_ref, ox_ref, gl_ref, bg_ref, wb_ref, wo_ref, o_ref):
    merged = jnp.zeros(x_ref.shape, F32)
    for r, br_ref in enumerate((od_ref, os_ref, ox_ref)):
        cs = slice(r * D_MODEL, (r + 1) * D_MODEL)
        gate = _sigmoid(gl_ref[:, cs] + bg_ref[:, cs])
        merged = merged + gate * _dot(br_ref[...], wb_ref[r])
    o_ref[...] = x_ref[...] + _dot(merged.astype(BF16), wo_ref[...])


def _merge(x2d, o_dn, o_sa, o_xa, gl, b_gate, w_branch, w_out, tm):
    rows, d = x2d.shape
    tm = min(tm, rows)
    row = pl.BlockSpec((tm, d), lambda i: (i, 0))
    return pl.pallas_call(
        _merge_kernel,
        out_shape=jax.ShapeDtypeStruct((rows, d), F32),
        grid=(rows // tm,),
        in_specs=[row, row, row, row,
                  pl.BlockSpec((tm, N_BRANCH * d), lambda i: (i, 0)),
                  pl.BlockSpec((1, N_BRANCH * d), lambda i: (0, 0)),
                  pl.BlockSpec((N_BRANCH, d, d), lambda i: (0, 0, 0)),
                  pl.BlockSpec((d, d), lambda i: (0, 0))],
        out_specs=row,
        compiler_params=_cparams(("parallel",)), name="merge",
    )(x2d, o_dn, o_sa, o_xa, gl, b_gate.reshape(1, N_BRANCH * d), w_branch, w_out)


def _ffn_kernel(x_ref, g_ref, wg_ref, wu_ref, wo_ref, o_ref, h_scr, acc_scr):
    f = pl.program_id(1)

    @pl.when(f == 0)
    def _():
        h_scr[...] = _rms(x_ref[...], g_ref[...]).astype(BF16)
        acc_scr[...] = x_ref[...]

    h = h_scr[...]
    gt = _dot(h, wg_ref[...])
    up = _dot(h, wu_ref[...])
    act = (gt * _sigmoid(gt) * up).astype(BF16)
    acc_scr[...] += _dot(act, wo_ref[...])

    @pl.when(f == pl.num_programs(1) - 1)
    def _():
        o_ref[...] = acc_scr[...]


def _ffn(x2d, g, wg, wu, wo, tm, tf):
    rows, d = x2d.shape
    tm = min(tm, rows)
    nf = wg.shape[1] // tf
    return pl.pallas_call(
        _ffn_kernel,
        out_shape=jax.ShapeDtypeStruct((rows, d), F32),
        grid=(rows // tm, nf),
        in_specs=[pl.BlockSpec((tm, d), lambda i, f: (i, 0)),
                  pl.BlockSpec((1, d), lambda i, f: (0, 0)),
                  pl.BlockSpec((d, tf), lambda i, f: (0, f)),
                  pl.BlockSpec((d, tf), lambda i, f: (0, f)),
                  pl.BlockSpec((tf, d), lambda i, f: (f, 0))],
        out_specs=pl.BlockSpec((tm, d), lambda i, f: (i, 0)),
        scratch_shapes=[pltpu.VMEM((tm, d), BF16), pltpu.VMEM((tm, d), F32)],
        compiler_params=_cparams(("parallel", "arbitrary")), name="ffn",
    )(x2d, g.reshape(1, d), wg, wu, wo)


def _final_norm_kernel(x_ref, g_ref, o_ref):
    o_ref[...] = _rms(x_ref[...], g_ref[...])


def _final_norm(x2d, g, tm):
    rows, d = x2d.shape
    tm = min(tm, rows)
    row = pl.BlockSpec((tm, d), lambda i: (i, 0))
    return pl.pallas_call(
        _final_norm_kernel, out_shape=jax.ShapeDtypeStruct((rows, d), F32), grid=(rows // tm,),
        in_specs=[row, pl.BlockSpec((1, d), lambda i: (0, 0))], out_specs=row,
        compiler_params=_cparams(("parallel",)), name="final_norm",
    )(x2d, g.reshape(1, d))


def _pad_cols(w, width):
    return jnp.pad(w, ((0, 0), (0, width - w.shape[1])))


def _layer(x2d, b, s, mem2d, tabs_sa, tabs_ix, norm_mix, norm_mem, w_in, b_gate, conv_w, a_log,
           dt_bias, dn_norm, w_mem_kv, w_branch, w_out, norm_ffn, w_ffn_in, w_ffn_out):
    wb = w_in.astype(BF16)
    w_small = _pad_cols(jnp.concatenate(
        [wb[:, OFF_DB:OFF_DB + DN_HEADS], wb[:, OFF_DA:OFF_DA + DN_HEADS],
         wb[:, OFF_IW:OFF_IW + IDX_HEADS]], axis=1), LANES)
    w_idx = _pad_cols(wb[:, OFF_IQ:OFF_IW], (IDX_HEADS + 2) * IDX_HD)

    dqkv, dz, small = _norm_matmul(
        x2d, norm_mix, [wb[:, OFF_DQKV:OFF_DZ], wb[:, OFF_DZ:OFF_DB], w_small],
        [F32, F32, F32], 256, "in_proj_dn")
    sqk, sv, idx = _norm_matmul(
        x2d, norm_mix, [wb[:, OFF_SQK:OFF_SV], wb[:, OFF_SV:OFF_IQ], w_idx],
        [F32, BF16, F32], 256, "in_proj_sa")
    xq, gl = _norm_matmul(
        x2d, norm_mix, [wb[:, OFF_XQ:OFF_GL], wb[:, OFF_GL:N_IN]], [BF16, F32], 256, "in_proj_xg")

    def r3(a):
        return a.reshape(b, s, a.shape[-1])

    dq, dk, dv, gb = _dn_prep(r3(dqkv), r3(small), conv_w, a_log, dt_bias, 256)
    o_dn = _dn_main(dq, dk, dv, r3(dz), gb, dn_norm)

    sq, sk, iq, ik, iw = _sa_prep(r3(sqk), r3(idx), r3(small), tabs_sa, tabs_ix, 256)
    ikt = jnp.swapaxes(ik[:, :, :IDX_HD], 1, 2)
    bias = _dsa_select(iq, ikt, iw, 128, min(TOPK_MAX, s // 4))
    o_sa = _dsa_attend(sq, sk, r3(sv), bias, 512)

    wkv = w_mem_kv.astype(BF16)
    mk, mv = _norm_matmul(mem2d, norm_mem, [wkv[:, :XA_WIDTH], wkv[:, XA_WIDTH:]],
                          [BF16, BF16], 256, "mem_kv")
    n_mem = mem2d.shape[0] // b
    o_xa = _mem_attend(r3(xq), mk.reshape(b, n_mem, XA_WIDTH), mv.reshape(b, n_mem, XA_WIDTH), 512)

    x2d = _merge(x2d, o_dn.reshape(b * s, -1), o_sa.reshape(b * s, -1), o_xa.reshape(b * s, -1),
                 gl, b_gate, w_branch.astype(BF16), w_out.astype(BF16), 256)
    wf = w_ffn_in.astype(BF16)
    return _ffn(x2d, norm_ffn, wf[:, :D_FF], wf[:, D_FF:], w_ffn_out.astype(BF16), 512, D_FF // 2)


def kernel(x, mem, positions, norm_mix, norm_mem, w_in, b_gate, conv_w, a_log, dt_bias, dn_norm,
           w_mem_kv, w_branch, w_out, norm_ffn, w_ffn_in, w_ffn_out, norm_final):
    b, s, d = x.shape
    assert d == D_MODEL and s % 512 == 0
    x2d = x.reshape(b * s, d)
    mem2d = mem.reshape(-1, d)
    tabs_sa = _rope_tables(positions, SA_HD, 512)
    tabs_ix = _rope_tables(positions, IDX_HD, 512)
    for l in range(w_in.shape[0]):
        x2d = _layer(x2d, b, s, mem2d, tabs_sa, tabs_ix, norm_mix[l], norm_mem[l], w_in[l],
                     b_gate[l], conv_w[l], a_log[l], dt_bias[l], dn_norm[l], w_mem_kv[l],
                     w_branch[l], w_out[l], norm_ffn[l], w_ffn_in[l], w_ffn_out[l])
    return _final_norm(x2d, norm_final, 512).reshape(b, s, d)
```

```python
import functools
import math

import jax
import jax.numpy as jnp
import numpy as np
from jax import lax
from jax.experimental import pallas as pl
from jax.experimental.pallas import tpu as pltpu

F32 = jnp.float32
BF16 = jnp.bfloat16
I16 = jnp.int16
I32 = jnp.int32

D_MODEL = 1024
CHUNK = 64
DN_HEADS = 8
DN_DK = 128
DN_WIDTH = DN_HEADS * DN_DK
CONV_K = 4
SA_HEADS = 8
SA_HD = 128
SA_WIDTH = SA_HEADS * SA_HD
IDX_HEADS = 8
IDX_HD = 64
TOPK_MAX = 256
XA_HEADS = 4
XA_HD = 256
XA_WIDTH = XA_HEADS * XA_HD
ROPE_THETA = 500000.0
ROPE_DIV = 4
D_FF = 2816
N_BRANCH = 3
NORM_EPS = 1e-6

LANES = 128
HALO = 8
VMEM_LIMIT = 48 * 1024 * 1024

OFF_DQKV = 0
OFF_DZ = 3 * DN_WIDTH
OFF_DB = OFF_DZ + DN_WIDTH
OFF_DA = OFF_DB + DN_HEADS
OFF_SQK = OFF_DA + DN_HEADS
OFF_SV = OFF_SQK + 2 * SA_WIDTH
OFF_IQ = OFF_SV + SA_WIDTH
OFF_IK = OFF_IQ + IDX_HEADS * IDX_HD
OFF_IW = OFF_IK + IDX_HD
OFF_XQ = OFF_IW + IDX_HEADS
OFF_GL = OFF_XQ + XA_WIDTH
N_IN = OFF_GL + N_BRANCH * D_MODEL

SM_DB = 0
SM_DA = DN_HEADS
SM_IW = 2 * DN_HEADS

NEG_BIAS = -1e30
INT_MIN = -2 ** 31
NEG_INF_KEY = -2 ** 31 + 0x7FFFFF
I16_MIN = -2 ** 15

SEL_TQ = 128
SEL_TK = 512
ATT_T = 512
ATT_ROWS = 128


def _cparams(sem):
    return pltpu.CompilerParams(dimension_semantics=sem, vmem_limit_bytes=VMEM_LIMIT)


def _rms(x, g):
    return x * lax.rsqrt(jnp.mean(x * x, axis=-1, keepdims=True) + NORM_EPS) * g


def _sigmoid(x):
    return 1.0 / (1.0 + jnp.exp(-x))


def _dot(a, b):
    return jnp.dot(a, b, preferred_element_type=F32)


def _dot_nt(a, b):
    return lax.dot_general(a, b, (((1,), (1,)), ((), ())), preferred_element_type=F32)


def _dot_tn(a, b):
    return lax.dot_general(a, b, (((0,), (0,)), ((), ())), preferred_element_type=F32)


def _dot_f32(a, b):
    return jnp.dot(a, b, preferred_element_type=F32, precision=lax.Precision.HIGHEST)


def _split(a):
    hi = a.astype(BF16)
    return hi, (a - hi.astype(F32)).astype(BF16)


def _dot_split(a, b):
    return _dot(a[0], b[0]) + (_dot(a[0], b[1]) + _dot(a[1], b[0]))


def _nmm_kernel(n, x_ref, g_ref, *refs):
    h = _rms(x_ref[...], g_ref[...]).astype(BF16)
    for w_ref, o_ref in zip(refs[:n], refs[n:]):
        o_ref[...] = _dot(h, w_ref[...]).astype(o_ref.dtype)


def _norm_matmul(x2d, g, weights, dtypes, tm, name):
    rows, d = x2d.shape
    tm = min(tm, rows)
    n = len(weights)
    in_specs = [pl.BlockSpec((tm, d), lambda i: (i, 0)), pl.BlockSpec((1, d), lambda i: (0, 0))]
    in_specs += [pl.BlockSpec(w.shape, lambda i: (0, 0)) for w in weights]
    out_specs = [pl.BlockSpec((tm, w.shape[1]), lambda i: (i, 0)) for w in weights]
    out_shape = [jax.ShapeDtypeStruct((rows, w.shape[1]), dt) for w, dt in zip(weights, dtypes)]
    return pl.pallas_call(
        functools.partial(_nmm_kernel, n),
        out_shape=out_shape, grid=(rows // tm,), in_specs=in_specs, out_specs=out_specs,
        compiler_params=_cparams(("parallel",)), name=name,
    )(x2d, g.reshape(1, d), *weights)


def _dnprep_kernel(x_ref, halo_ref, cw_ref, sm_ref, al_ref, dt_ref, q_ref, k_ref, v_ref, gb_ref):
    i = pl.program_id(1)
    t = x_ref.shape[1]
    keep = (i > 0).astype(F32)
    for s, o_ref in enumerate((q_ref, k_ref, v_ref)):
        cols = slice(s * DN_WIDTH, (s + 1) * DN_WIDTH)
        xc = jnp.concatenate([halo_ref[0, :, cols] * keep, x_ref[0, :, cols]], axis=0)
        y = jnp.zeros((t, DN_WIDTH), F32)
        for j in range(CONV_K):
            lo = HALO - (CONV_K - 1) + j
            y = y + xc[lo:lo + t] * cw_ref[j:j + 1, cols]
        y = y * _sigmoid(y)
        if s == 2:
            o_ref[0] = y
        else:
            for h in range(DN_HEADS):
                hs = slice(h * DN_DK, (h + 1) * DN_DK)
                seg = y[:, hs]
                seg = seg * lax.rsqrt(jnp.sum(seg * seg, axis=-1, keepdims=True) + NORM_EPS)
                if s == 0:
                    seg = seg * DN_DK ** -0.5
                o_ref[0, :, hs] = seg
    sm = sm_ref[0]
    beta = _sigmoid(sm)
    z = sm + dt_ref[...]
    softplus = jnp.maximum(z, 0.0) + jnp.log1p(jnp.exp(-jnp.abs(z)))
    g = -jnp.exp(al_ref[...]) * softplus
    lane = lax.broadcasted_iota(I32, sm.shape, 1)
    gb_ref[0] = jnp.where(lane < SM_DA, beta, g)


def _dn_prep(qkv, small, conv_w, a_log, dt_bias, t):
    b, s, c = qkv.shape
    t = min(t, s)
    al = jnp.zeros((1, LANES), F32).at[0, SM_DA:SM_DA + DN_HEADS].set(a_log)
    dt = jnp.zeros((1, LANES), F32).at[0, SM_DA:SM_DA + DN_HEADS].set(dt_bias)
    hb = t // HALO
    out_sd = jax.ShapeDtypeStruct((b, s, DN_WIDTH), F32)
    o_spec = pl.BlockSpec((1, t, DN_WIDTH), lambda bi, i: (bi, i, 0))
    return pl.pallas_call(
        _dnprep_kernel,
        out_shape=[out_sd, out_sd, out_sd, jax.ShapeDtypeStruct((b, s, LANES), F32)],
        grid=(b, s // t),
        in_specs=[
            pl.BlockSpec((1, t, c), lambda bi, i: (bi, i, 0)),
            pl.BlockSpec((1, HALO, c), lambda bi, i: (bi, jnp.maximum(i * hb - 1, 0), 0)),
            pl.BlockSpec((CONV_K, c), lambda bi, i: (0, 0)),
            pl.BlockSpec((1, t, LANES), lambda bi, i: (bi, i, 0)),
            pl.BlockSpec((1, LANES), lambda bi, i: (0, 0)),
            pl.BlockSpec((1, LANES), lambda bi, i: (0, 0)),
        ],
        out_specs=[o_spec, o_spec, o_spec, pl.BlockSpec((1, t, LANES), lambda bi, i: (bi, i, 0))],
        compiler_params=_cparams(("parallel", "parallel")), name="dn_prep",
    )(qkv, qkv, conv_w, small, al, dt)


def _dn_kernel(q_ref, k_ref, v_ref, z_ref, gb_ref, nrm_ref, o_ref, s_scr):
    @pl.when(pl.program_id(1) == 0)
    def _():
        s_scr[...] = jnp.zeros_like(s_scr)

    c = CHUNK
    heads = range(DN_HEADS)
    hs = [slice(h * DN_DK, (h + 1) * DN_DK) for h in heads]
    row = lax.broadcasted_iota(I32, (c, c), 0)
    col = lax.broadcasted_iota(I32, (c, c), 1)
    incl = row >= col
    strict = row > col
    tri = incl.astype(F32)
    eye = (row == col).astype(F32)
    gb = gb_ref[0]
    gc_all = _dot_f32(tri, gb)
    gc_t = gc_all.T

    q = [q_ref[0, :, hs[h]] for h in heads]
    k = [k_ref[0, :, hs[h]] for h in heads]
    beta = [gb[:, SM_DB + h:SM_DB + h + 1] for h in heads]
    gcol = [gc_all[:, SM_DA + h:SM_DA + h + 1] for h in heads]
    glast = [gc_all[c - 1:c, SM_DA + h:SM_DA + h + 1] for h in heads]
    decay = [jnp.where(incl, jnp.exp(jnp.where(incl, gcol[h] - gc_t[SM_DA + h:SM_DA + h + 1, :],
                                               0.0)), 0.0) for h in heads]
    kb = [k[h] * beta[h] for h in heads]
    kbf = [k[h].astype(BF16) for h in heads]
    m = [jnp.where(strict, _dot_nt(kb[h].astype(BF16), kbf[h]) * decay[h], 0.0) for h in heads]
    x = [-m[h] for h in heads]
    p = [eye + x[h] for h in heads]
    xs = [_split(x[h]) for h in heads]
    for _ in range(5):
        x = [_dot_split(xs[h], xs[h]) for h in heads]
        xs = [_split(x[h]) for h in heads]
        p = [p[h] + _dot_split(_split(p[h]), xs[h]) for h in heads]
    eg = [jnp.exp(gcol[h]) for h in heads]
    rhs = [jnp.concatenate([v_ref[0, :, hs[h]] * beta[h], kb[h] * eg[h]], axis=1) for h in heads]
    sol = [_dot_split(_split(p[h]), _split(rhs[h])) for h in heads]
    aqk = [(_dot_nt(q[h].astype(BF16), kbf[h]) * decay[h]).astype(BF16) for h in heads]
    qd = [(q[h] * eg[h]).astype(BF16) for h in heads]
    kd = [(k[h] * jnp.exp(glast[h] - gcol[h])).astype(BF16) for h in heads]
    st = [s_scr[h] for h in heads]
    stb = [st[h].astype(BF16) for h in heads]
    vnew = [sol[h][:, :DN_DK] - _dot(sol[h][:, DN_DK:].astype(BF16), stb[h]) for h in heads]
    vnb = [vnew[h].astype(BF16) for h in heads]
    o = [_dot(qd[h], stb[h]) + _dot(aqk[h], vnb[h]) for h in heads]
    for h in heads:
        s_scr[h] = st[h] * jnp.exp(glast[h]) + _dot_tn(kd[h], vnb[h])
    for h in heads:
        zh = z_ref[0, :, hs[h]]
        o_ref[0, :, hs[h]] = (_rms(o[h], nrm_ref[...]) * (zh * _sigmoid(zh))).astype(o_ref.dtype)


def _dn_main(q, k, v, z, gb, dn_norm):
    b, s, _ = q.shape
    blk = pl.BlockSpec((1, CHUNK, DN_WIDTH), lambda bi, i: (bi, i, 0))
    return pl.pallas_call(
        _dn_kernel,
        out_shape=jax.ShapeDtypeStruct((b, s, DN_WIDTH), BF16),
        grid=(b, s // CHUNK),
        in_specs=[blk, blk, blk, blk,
                  pl.BlockSpec((1, CHUNK, LANES), lambda bi, i: (bi, i, 0)),
                  pl.BlockSpec((1, DN_DK), lambda bi, i: (0, 0))],
        out_specs=blk,
        scratch_shapes=[pltpu.VMEM((DN_HEADS, DN_DK, DN_DK), F32)],
        compiler_params=_cparams(("parallel", "arbitrary")), name="dn_main",
    )(q, k, v, z, gb, dn_norm.reshape(1, DN_DK))


def _rope_tab_kernel(pos_ref, f_ref, sg_ref, cs_ref, sn_ref):
    ang = pos_ref[0].astype(F32) * f_ref[...]
    cs_ref[0] = jnp.cos(ang)
    sn_ref[0] = jnp.sin(ang) * sg_ref[...]


def _lane_tables(head_dim):
    rd = head_dim // ROPE_DIV
    half = rd // 2
    inv_freq = ROPE_THETA ** (-(jnp.arange(half, dtype=F32) * 2.0 / rd))
    lane = np.arange(LANES) % head_dim
    sel = np.where(lane < rd, lane % half, 0)
    freq = jnp.where(jnp.asarray(lane < rd), inv_freq[sel], 0.0).reshape(1, LANES)
    sign = np.where(lane < half, -1.0, np.where(lane < rd, 1.0, 0.0)).astype(np.float32)
    return freq.astype(F32), jnp.asarray(sign).reshape(1, LANES)


def _rope_tables(positions, head_dim, t):
    b, s = positions.shape
    t = min(t, s)
    freq, sign = _lane_tables(head_dim)
    sd = jax.ShapeDtypeStruct((b, s, LANES), F32)
    blk = pl.BlockSpec((1, t, LANES), lambda bi, i: (bi, i, 0))
    vec = pl.BlockSpec((1, LANES), lambda bi, i: (0, 0))
    return pl.pallas_call(
        _rope_tab_kernel, out_shape=[sd, sd], grid=(b, s // t),
        in_specs=[pl.BlockSpec((1, t, 1), lambda bi, i: (bi, i, 0)), vec, vec],
        out_specs=[blk, blk],
        compiler_params=_cparams(("parallel", "parallel")), name="rope_tables",
    )(positions.reshape(b, s, 1), freq, sign)


def _rope(x, cs, sn, first_half, half):
    partner = jnp.where(first_half, pltpu.roll(x, LANES - half, 1), pltpu.roll(x, half, 1))
    return x * cs + partner * sn


def _saprep_kernel(qk_ref, ix_ref, sm_ref, cs_ref, sn_ref, ci_ref, si_ref,
                   q_ref, k_ref, iq_ref, ik_ref, iw_ref):
    t = qk_ref.shape[1]
    lane = lax.broadcasted_iota(I32, (t, LANES), 1)
    half_sa = SA_HD // ROPE_DIV // 2
    half_ix = IDX_HD // ROPE_DIV // 2
    fh_sa = lane < half_sa
    fh_ix = (lane % IDX_HD) < half_ix
    cs, sn = cs_ref[0], sn_ref[0]
    ci, si = ci_ref[0], si_ref[0]
    q_scale = SA_HD ** -0.5 * math.log2(math.e)
    for h in range(SA_HEADS):
        hs = slice(h * SA_HD, (h + 1) * SA_HD)
        q_ref[0, :, hs] = (_rope(qk_ref[0, :, hs], cs, sn, fh_sa, half_sa) * q_scale).astype(BF16)
        ks = slice(SA_WIDTH + h * SA_HD, SA_WIDTH + (h + 1) * SA_HD)
        k_ref[0, :, hs] = _rope(qk_ref[0, :, ks], cs, sn, fh_sa, half_sa).astype(BF16)
    for j in range(IDX_HEADS * IDX_HD // LANES):
        r = _rope(ix_ref[0, :, j * LANES:(j + 1) * LANES], ci, si, fh_ix, half_ix).astype(BF16)
        iq_ref[0, 2 * j] = r[:, :IDX_HD]
        iq_ref[0, 2 * j + 1] = r[:, IDX_HD:]
    nq = IDX_HEADS * IDX_HD
    ik_ref[0] = _rope(ix_ref[0, :, nq:nq + LANES], ci, si, fh_ix, half_ix).astype(BF16)
    iw_ref[0] = sm_ref[0] * (IDX_HEADS ** -0.5 * IDX_HD ** -0.5)


def _sa_prep(sqk, idx, small, tabs_sa, tabs_ix, t):
    b, s, _ = sqk.shape
    t = min(t, s)
    tab = pl.BlockSpec((1, t, LANES), lambda bi, i: (bi, i, 0))
    wide = pl.BlockSpec((1, t, SA_WIDTH), lambda bi, i: (bi, i, 0))
    return pl.pallas_call(
        _saprep_kernel,
        out_shape=[jax.ShapeDtypeStruct((b, s, SA_WIDTH), BF16),
                   jax.ShapeDtypeStruct((b, s, SA_WIDTH), BF16),
                   jax.ShapeDtypeStruct((b, IDX_HEADS, s, IDX_HD), BF16),
                   jax.ShapeDtypeStruct((b, s, LANES), BF16),
                   jax.ShapeDtypeStruct((b, s, LANES), F32)],
        grid=(b, s // t),
        in_specs=[pl.BlockSpec((1, t, 2 * SA_WIDTH), lambda bi, i: (bi, i, 0)),
                  pl.BlockSpec((1, t, idx.shape[2]), lambda bi, i: (bi, i, 0)),
                  tab, tab, tab, tab, tab],
        out_specs=[wide, wide,
                   pl.BlockSpec((1, IDX_HEADS, t, IDX_HD), lambda bi, i: (bi, 0, i, 0)),
                   tab, tab],
        compiler_params=_cparams(("parallel", "parallel")), name="sa_prep",
    )(sqk, idx, small, *tabs_sa, *tabs_ix)


def _select_kernel(iq_ref, ikt_ref, iw_ref, bias_ref, key_scr, hi_scr, lo_scr, wb_scr,
                   *, tq, tk, ksel, seq):
    i = pl.program_id(1)
    nkt = ((i + 1) * tq + tk - 1) // tk
    nsub = tk // LANES
    iw = iw_ref[0]
    for h in range(IDX_HEADS):
        wb_scr[h] = jnp.broadcast_to(iw[:, SM_IW + h:SM_IW + h + 1], (tq, LANES))
    q_chunk = (i * tq + lax.broadcasted_iota(I32, (tq, 1), 0)) // CHUNK
    col0 = lax.broadcasted_iota(I32, (1, tk), 1)

    def score_tile(kt, carry):
        off = pl.multiple_of(kt * tk, tk)
        ikt = ikt_ref[0, :, pl.ds(off, tk)]
        acc = [jnp.zeros((tq, LANES), F32) for _ in range(nsub)]
        for h in range(IDX_HEADS):
            d = _dot(iq_ref[0, h], ikt)
            w = wb_scr[h]
            for j in range(nsub):
                acc[j] = acc[j] + jnp.maximum(d[:, j * LANES:(j + 1) * LANES], 0.0) * w
        adm = ((off + col0) // CHUNK) <= q_chunk
        sc = jnp.where(adm, jnp.concatenate(acc, axis=1), -jnp.inf)
        sc = jnp.where(sc == 0.0, 0.0, sc)
        bits = pltpu.bitcast(sc, I32)
        key = bits ^ ((bits >> 31) & 0x7FFFFFFF)
        key_scr[:, pl.ds(off, tk)] = key
        hi_scr[:, pl.ds(off, tk)] = (key >> 16).astype(I16)
        return carry

    lax.fori_loop(0, nkt, score_tile, 0)

    def count32(pred):
        def body(kt, c):
            off = pl.multiple_of(kt * tk, tk)
            hit = jnp.where(pred(key_scr[:, pl.ds(off, tk)], off), 1.0, 0.0)
            for j in range(nsub):
                c = c + hit[:, j * LANES:(j + 1) * LANES]
            return c
        c = lax.fori_loop(0, nkt, body, jnp.zeros((tq, LANES), F32))
        return jnp.sum(c, axis=1, keepdims=True)

    one16 = jnp.ones((tq, LANES), I16)
    zero16 = jnp.zeros((tq, LANES), I16)

    def count16(scr, pred):
        def body(kt, acc):
            off = kt * tk
            for j in range(nsub):
                blk = scr[:, pl.ds(pl.multiple_of(off + j * LANES, LANES), LANES)]
                acc = acc + jnp.where(pred(blk), one16, zero16)
            return acc
        acc = lax.fori_loop(0, nkt, body, zero16)
        return jnp.sum(acc.astype(I32).astype(F32), axis=1, keepdims=True)

    def lanes16(v):
        return jnp.broadcast_to(v, (tq, LANES)).astype(I16)

    def kth16(scr, target):
        z = lanes16(jnp.zeros((tq, 1), I32))
        t0 = jnp.where(count16(scr, lambda blk: blk >= z) >= target, 0, I16_MIN).astype(I32)

        def step(b, t):
            cand = t | jnp.left_shift(jnp.int32(1), 14 - b)
            cb = lanes16(cand)
            return jnp.where(count16(scr, lambda blk: blk >= cb) >= target, cand, t)

        return lax.fori_loop(0, 15, step, t0)

    kf = float(ksel)
    t_hi = kth16(hi_scr, kf)
    thb = lanes16(t_hi)
    need_lo = kf - count16(hi_scr, lambda blk: blk > thb)

    def build_lo(kt, carry):
        off = pl.multiple_of(kt * tk, tk)
        key = key_scr[:, pl.ds(off, tk)]
        low = (key & 0xFFFF) + I16_MIN
        lo_scr[:, pl.ds(off, tk)] = jnp.where((key >> 16) == t_hi, low, I16_MIN).astype(I16)
        return carry

    lax.fori_loop(0, nkt, build_lo, 0)
    t_lo = kth16(lo_scr, need_lo)
    thr = (t_hi * 65536) | ((t_lo - I16_MIN) & 0xFFFF)

    finite = thr > NEG_INF_KEY
    thr_eff = jnp.maximum(thr, NEG_INF_KEY + 1)
    n_ge = count32(lambda keys, off: keys >= thr_eff)
    tie_rows = jnp.where(finite & (n_ge > kf), 1.0, 0.0)
    has_tie = jnp.max(tie_rows) > 0.0

    bias_ref[...] = jnp.full(bias_ref.shape, NEG_BIAS, bias_ref.dtype)

    @pl.when(jnp.logical_not(has_tie))
    def _():
        def write(kt, carry):
            off = pl.multiple_of(kt * tk, tk)
            sel = key_scr[:, pl.ds(off, tk)] >= thr_eff
            bias_ref[0, :, pl.ds(off, tk)] = jnp.where(sel, 0.0, NEG_BIAS).astype(bias_ref.dtype)
            return carry
        lax.fori_loop(0, nkt, write, 0)

    @pl.when(has_tie)
    def _():
        need = kf - count32(lambda keys, off: keys > thr)
        nbits = max(1, int(np.ceil(np.log2(seq))))

        def idx_step(b, j):
            cand = j | jnp.left_shift(jnp.int32(1), nbits - 1 - b)
            below = count32(lambda keys, off: (keys == thr) & ((off + col0) < cand))
            return jnp.where(below < need, cand, j)

        cut = lax.fori_loop(0, nbits, idx_step, jnp.zeros((tq, 1), I32))

        def write(kt, carry):
            off = pl.multiple_of(kt * tk, tk)
            keys = key_scr[:, pl.ds(off, tk)]
            tie_ok = (keys == thr) & ((off + col0) <= cut)
            sel = (keys >= thr_eff) & ((keys > thr) | tie_ok | jnp.logical_not(finite))
            bias_ref[0, :, pl.ds(off, tk)] = jnp.where(sel, 0.0, NEG_BIAS).astype(bias_ref.dtype)
            return carry
        lax.fori_loop(0, nkt, write, 0)


def _dsa_select(iq, ikt, iw, ksel):
    b, _, s, _ = iq.shape
    tq, tk = min(SEL_TQ, s), min(SEL_TK, s)
    assert s % tk == 0 and tk % tq == 0 and s // LANES < 2 ** 15
    return pl.pallas_call(
        functools.partial(_select_kernel, tq=tq, tk=tk, ksel=ksel, seq=s),
        out_shape=jax.ShapeDtypeStruct((b, s, s), BF16),
        grid=(b, s // tq),
        in_specs=[pl.BlockSpec((1, IDX_HEADS, tq, IDX_HD), lambda bi, i: (bi, 0, i, 0)),
                  pl.BlockSpec((1, IDX_HD, s), lambda bi, i: (bi, 0, 0)),
                  pl.BlockSpec((1, tq, LANES), lambda bi, i: (bi, i, 0))],
        out_specs=pl.BlockSpec((1, tq, s), lambda bi, i: (bi, i, 0)),
        scratch_shapes=[pltpu.VMEM((tq, s), I32), pltpu.VMEM((tq, s), I16),
                        pltpu.VMEM((tq, s), I16), pltpu.VMEM((IDX_HEADS, tq, LANES), F32)],
        compiler_params=_cparams(("parallel", "parallel")), name="dsa_select",
    )(iq, ikt, iw)


def _attn_kernel(it_ref, jt_ref, q_ref, kt_ref, v_ref, b_ref, o_ref, *state, rows):
    m_scr = state[:SA_HEADS]
    acc_scr = state[SA_HEADS:]
    p = pl.program_id(1)
    i = it_ref[p]
    j = jt_ref[p]
    t = q_ref.shape[1]
    tk = kt_ref.shape[3]

    @pl.when(j == 0)
    def _():
        for h in range(SA_HEADS):
            m_scr[h][...] = jnp.full(m_scr[h].shape, NEG_BIAS, F32)
            acc_scr[h][...] = jnp.zeros_like(acc_scr[h])

    ones = jnp.ones((tk, SA_HD), BF16)
    eye = (lax.broadcasted_iota(I32, (rows, rows), 0)
           == lax.broadcasted_iota(I32, (rows, rows), 1)).astype(BF16)

    def update(r, carry):
        rs = pl.ds(pl.multiple_of(r * rows, rows), rows)
        heads = range(SA_HEADS)
        hs = [slice(h * SA_HD, (h + 1) * SA_HD) for h in heads]
        mask = b_ref[0, rs, :]
        s = [_dot(jnp.concatenate([q_ref[0, rs, hs[h]], eye], axis=1),
                  jnp.concatenate([kt_ref[0, h], mask], axis=0)) for h in heads]
        m_old = [m_scr[h][rs, :] for h in heads]
        m_new = [jnp.maximum(m_old[h], jnp.max(s[h], axis=1, keepdims=True)) for h in heads]
        pr = [jnp.exp2(s[h] - jnp.concatenate([m_new[h]] * (tk // LANES), axis=1)).astype(BF16)
              for h in heads]
        pv = [_dot(pr[h], jnp.concatenate([v_ref[0, :, hs[h]], ones], axis=1)) for h in heads]
        for h in heads:
            alpha = jnp.exp2(m_old[h] - m_new[h])
            acc_scr[h][rs, :] = acc_scr[h][rs, :] * jnp.concatenate([alpha, alpha], axis=1) + pv[h]
            m_scr[h][rs, :] = m_new[h]
        return carry

    lax.fori_loop(0, t // rows, update, 0)

    @pl.when(j == i)
    def _():
        for h in range(SA_HEADS):
            hs = slice(h * SA_HD, (h + 1) * SA_HD)
            o_ref[0, :, hs] = (acc_scr[h][:, :SA_HD] / acc_scr[h][:, SA_HD:]).astype(o_ref.dtype)


def _dsa_attend(q, k, v, bias):
    b, s, w = q.shape
    t = min(ATT_T, s)
    n = s // t
    pairs = [(i, j) for i in range(n) for j in range(i + 1)]
    it = jnp.asarray([p[0] for p in pairs], I32)
    jt = jnp.asarray([p[1] for p in pairs], I32)
    grid_spec = pltpu.PrefetchScalarGridSpec(
        num_scalar_prefetch=2, grid=(b, len(pairs)),
        in_specs=[pl.BlockSpec((1, t, w), lambda bi, p, it, jt: (bi, it[p], 0)),
                  pl.BlockSpec((1, SA_HEADS, SA_HD, t), lambda bi, p, it, jt: (bi, 0, 0, jt[p])),
                  pl.BlockSpec((1, t, w), lambda bi, p, it, jt: (bi, jt[p], 0)),
                  pl.BlockSpec((1, t, t), lambda bi, p, it, jt: (bi, it[p], jt[p]))],
        out_specs=pl.BlockSpec((1, t, w), lambda bi, p, it, jt: (bi, it[p], 0)),
        scratch_shapes=([pltpu.VMEM((t, LANES), F32) for _ in range(SA_HEADS)]
                        + [pltpu.VMEM((t, 2 * SA_HD), F32) for _ in range(SA_HEADS)]))
    return pl.pallas_call(
        functools.partial(_attn_kernel, rows=min(ATT_ROWS, t)),
        out_shape=jax.ShapeDtypeStruct((b, s, w), BF16), grid_spec=grid_spec,
        compiler_params=_cparams(("parallel", "arbitrary")), name="dsa_attend",
    )(it, jt, q, k, v, bias)


def _xattn_kernel(q_ref, mk_ref, mv_ref, o_ref):
    for h in range(XA_HEADS):
        hs = slice(h * XA_HD, (h + 1) * XA_HD)
        s = _dot_nt(q_ref[0, :, hs], mk_ref[0, :, hs]) * XA_HD ** -0.5
        p = jnp.exp(s - jnp.max(s, axis=1, keepdims=True))
        o = _dot(p.astype(BF16), mv_ref[0, :, hs]) / jnp.sum(p, axis=1, keepdims=True)
        o_ref[0, :, hs] = o.astype(o_ref.dtype)


def _mem_attend(xq, mk, mv, t):
    b, s, w = xq.shape
    t = min(t, s)
    m = mk.shape[1]
    return pl.pallas_call(
        _xattn_kernel,
        out_shape=jax.ShapeDtypeStruct((b, s, w), BF16),
        grid=(b, s // t),
        in_specs=[pl.BlockSpec((1, t, w), lambda bi, i: (bi, i, 0)),
                  pl.BlockSpec((1, m, w), lambda bi, i: (bi, 0, 0)),
                  pl.BlockSpec((1, m, w), lambda bi, i: (bi, 0, 0))],
        out_specs=pl.BlockSpec((1, t, w), lambda bi, i: (bi, i, 0)),
        compiler_params=_cparams(("parallel", "parallel")), name="mem_attend",
    )(xq, mk, mv)


def _merge_kernel(x_ref, od_ref, os_ref, ox_ref, gl_ref, bg_ref, wb_ref, wo_ref, o_ref):
    merged = jnp.zeros(x_ref.shape, F32)
    for r, br_ref in enumerate((od_ref, os_ref, ox_ref)):
        cs = slice(r * D_MODEL, (r + 1) * D_MODEL)
        gate = _sigmoid(gl_ref[:, cs] + bg_ref[:, cs])
        merged = merged + gate * _dot(br_ref[...], wb_ref[r])
    o_ref[...] = x_ref[...] + _dot(merged.astype(BF16), wo_ref[...])


def _merge(x2d, o_dn, o_sa, o_xa, gl, b_gate, w_branch, w_out, tm):
    rows, d = x2d.shape
    tm = min(tm, rows)
    row = pl.BlockSpec((tm, d), lambda i: (i, 0))
    return pl.pallas_call(
        _merge_kernel,
        out_shape=jax.ShapeDtypeStruct((rows, d), F32),
        grid=(rows // tm,),
        in_specs=[row, row, row, row,
                  pl.BlockSpec((tm, N_BRANCH * d), lambda i: (i, 0)),
                  pl.BlockSpec((1, N_BRANCH * d), lambda i: (0, 0)),
                  pl.BlockSpec((N_BRANCH, d, d), lambda i: (0, 0, 0)),
                  pl.BlockSpec((d, d), lambda i: (0, 0))],
        out_specs=row,
        compiler_params=_cparams(("parallel",)), name="merge",
    )(x2d, o_dn, o_sa, o_xa, gl, b_gate.reshape(1, N_BRANCH * d), w_branch, w_out)


def _ffn_kernel(x_ref, g_ref, wg_ref, wu_ref, wo_ref, o_ref, h_scr, acc_scr):
    f = pl.program_id(1)

    @pl.when(f == 0)
    def _():
        h_scr[...] = _rms(x_ref[...], g_ref[...]).astype(BF16)
        acc_scr[...] = x_ref[...]

    h = h_scr[...]
    gt = _dot(h, wg_ref[...])
    up = _dot(h, wu_ref[...])
    act = (gt * _sigmoid(gt) * up).astype(BF16)
    acc_scr[...] += _dot(act, wo_ref[...])

    @pl.when(f == pl.num_programs(1) - 1)
    def _():
        o_ref[...] = acc_scr[...]


def _ffn(x2d, g, wg, wu, wo, tm, tf):
    rows, d = x2d.shape
    tm = min(tm, rows)
    nf = wg.shape[1] // tf
    return pl.pallas_call(
        _ffn_kernel,
        out_shape=jax.ShapeDtypeStruct((rows, d), F32),
        grid=(rows // tm, nf),
        in_specs=[pl.BlockSpec((tm, d), lambda i, f: (i, 0)),
                  pl.BlockSpec((1, d), lambda i, f: (0, 0)),
                  pl.BlockSpec((d, tf), lambda i, f: (0, f)),
                  pl.BlockSpec((d, tf), lambda i, f: (0, f)),
                  pl.BlockSpec((tf, d), lambda i, f: (f, 0))],
        out_specs=pl.BlockSpec((tm, d), lambda i, f: (i, 0)),
        scratch_shapes=[pltpu.VMEM((tm, d), BF16), pltpu.VMEM((tm, d), F32)],
        compiler_params=_cparams(("parallel", "arbitrary")), name="ffn",
    )(x2d, g.reshape(1, d), wg, wu, wo)


def _final_norm_kernel(x_ref, g_ref, o_ref):
    o_ref[...] = _rms(x_ref[...], g_ref[...])


def _final_norm(x2d, g, tm):
    rows, d = x2d.shape
    tm = min(tm, rows)
    row = pl.BlockSpec((tm, d), lambda i: (i, 0))
    return pl.pallas_call(
        _final_norm_kernel, out_shape=jax.ShapeDtypeStruct((rows, d), F32), grid=(rows // tm,),
        in_specs=[row, pl.BlockSpec((1, d), lambda i: (0, 0))], out_specs=row,
        compiler_params=_cparams(("parallel",)), name="final_norm",
    )(x2d, g.reshape(1, d))


def _pad_cols(w, width):
    return jnp.pad(w, ((0, 0), (0, width - w.shape[1])))


def _layer(x2d, b, s, mem2d, tabs_sa, tabs_ix, norm_mix, norm_mem, w_in, b_gate, conv_w, a_log,
           dt_bias, dn_norm, w_mem_kv, w_branch, w_out, norm_ffn, w_ffn_in, w_ffn_out):
    wb = w_in.astype(BF16)
    w_small = _pad_cols(jnp.concatenate(
        [wb[:, OFF_DB:OFF_DB + DN_HEADS], wb[:, OFF_DA:OFF_DA + DN_HEADS],
         wb[:, OFF_IW:OFF_IW + IDX_HEADS]], axis=1), LANES)
    w_idx = _pad_cols(wb[:, OFF_IQ:OFF_IW], (IDX_HEADS + 2) * IDX_HD)

    dqkv, dz, small = _norm_matmul(
        x2d, norm_mix, [wb[:, OFF_DQKV:OFF_DZ], wb[:, OFF_DZ:OFF_DB], w_small],
        [F32, F32, F32], 256, "in_proj_dn")
    sqk, sv, idx = _norm_matmul(
        x2d, norm_mix, [wb[:, OFF_SQK:OFF_SV], wb[:, OFF_SV:OFF_IQ], w_idx],
        [F32, BF16, F32], 256, "in_proj_sa")
    xq, gl = _norm_matmul(
        x2d, norm_mix, [wb[:, OFF_XQ:OFF_GL], wb[:, OFF_GL:N_IN]], [BF16, F32], 256, "in_proj_xg")

    def r3(a):
        return a.reshape(b, s, a.shape[-1])

    dq, dk, dv, gb = _dn_prep(r3(dqkv), r3(small), conv_w, a_log, dt_bias, 256)
    o_dn = _dn_main(dq, dk, dv, r3(dz), gb, dn_norm)

    sq, sk, iq, ik, iw = _sa_prep(r3(sqk), r3(idx), r3(small), tabs_sa, tabs_ix, 256)
    ikt = jnp.swapaxes(ik[:, :, :IDX_HD], 1, 2)
    bias = _dsa_select(iq, ikt, iw, min(TOPK_MAX, s // 4))
    skt = jnp.transpose(sk.reshape(b, s, SA_HEADS, SA_HD), (0, 2, 3, 1))
    o_sa = _dsa_attend(sq, skt, r3(sv), bias)

    wkv = w_mem_kv.astype(BF16)
    mk, mv = _norm_matmul(mem2d, norm_mem, [wkv[:, :XA_WIDTH], wkv[:, XA_WIDTH:]],
                          [BF16, BF16], 256, "mem_kv")
    n_mem = mem2d.shape[0] // b
    o_xa = _mem_attend(r3(xq), mk.reshape(b, n_mem, XA_WIDTH), mv.reshape(b, n_mem, XA_WIDTH), 512)

    x2d = _merge(x2d, o_dn.reshape(b * s, -1), o_sa.reshape(b * s, -1), o_xa.reshape(b * s, -1),
                 gl, b_gate, w_branch.astype(BF16), w_out.astype(BF16), 256)
    wf = w_ffn_in.astype(BF16)
    return _ffn(x2d, norm_ffn, wf[:, :D_FF], wf[:, D_FF:], w_ffn_out.astype(BF16), 512, D_FF // 2)


def kernel(x, mem, positions, norm_mix, norm_mem, w_in, b_gate, conv_w, a_log, dt_bias, dn_norm,
           w_mem_kv, w_branch, w_out, norm_ffn, w_ffn_in, w_ffn_out, norm_final):
    b, s, d = x.shape
    assert d == D_MODEL and s % ATT_T == 0
    x2d = x.reshape(b * s, d)
    mem2d = mem.reshape(-1, d)
    tabs_sa = _rope_tables(positions, SA_HD, 512)
    tabs_ix = _rope_tables(positions, IDX_HD, 512)
    for l in range(w_in.shape[0]):
        x2d = _layer(x2d, b, s, mem2d, tabs_sa, tabs_ix, norm_mix[l], norm_mem[l], w_in[l],
                     b_gate[l], conv_w[l], a_log[l], dt_bias[l], dn_norm[l], w_mem_kv[l],
                     w_branch[l], w_out[l], norm_ffn[l], w_ffn_in[l], w_ffn_out[l])
    return _final_norm(x2d, norm_final, 512).reshape(b, s, d)
```

```python
import functools
import math

import jax
import jax.numpy as jnp
import numpy as np
from jax import lax
from jax.experimental import pallas as pl
from jax.experimental.pallas import tpu as pltpu

F32 = jnp.float32
BF16 = jnp.bfloat16
I16 = jnp.int16
I32 = jnp.int32

D_MODEL = 1024
CHUNK = 64
DN_HEADS = 8
DN_DK = 128
DN_WIDTH = DN_HEADS * DN_DK
CONV_K = 4
SA_HEADS = 8
SA_HD = 128
SA_WIDTH = SA_HEADS * SA_HD
IDX_HEADS = 8
IDX_HD = 64
TOPK_MAX = 256
XA_HEADS = 4
XA_HD = 256
XA_WIDTH = XA_HEADS * XA_HD
ROPE_THETA = 500000.0
ROPE_DIV = 4
D_FF = 2816
N_BRANCH = 3
NORM_EPS = 1e-6

LANES = 128
HALO = 8
VMEM_LIMIT = 48 * 1024 * 1024

OFF_DQKV = 0
OFF_DZ = 3 * DN_WIDTH
OFF_DB = OFF_DZ + DN_WIDTH
OFF_DA = OFF_DB + DN_HEADS
OFF_SQK = OFF_DA + DN_HEADS
OFF_SV = OFF_SQK + 2 * SA_WIDTH
OFF_IQ = OFF_SV + SA_WIDTH
OFF_IK = OFF_IQ + IDX_HEADS * IDX_HD
OFF_IW = OFF_IK + IDX_HD
OFF_XQ = OFF_IW + IDX_HEADS
OFF_GL = OFF_XQ + XA_WIDTH
N_IN = OFF_GL + N_BRANCH * D_MODEL

SM_DB = 0
SM_DA = DN_HEADS
SM_IW = 2 * DN_HEADS

NEG_BIAS = -1e30
INT_MIN = -2 ** 31
NEG_INF_KEY = -2 ** 31 + 0x7FFFFF
I16_MIN = -2 ** 15

SEL_TQ = 256
SEL_SUB = 128
SEL_TK = 512
ATT_T = 512
ATT_ROWS = 128


def _cparams(sem):
    return pltpu.CompilerParams(dimension_semantics=sem, vmem_limit_bytes=VMEM_LIMIT)


def _rms(x, g):
    return x * lax.rsqrt(jnp.mean(x * x, axis=-1, keepdims=True) + NORM_EPS) * g


def _sigmoid(x):
    return 1.0 / (1.0 + jnp.exp(-x))


def _dot(a, b):
    return jnp.dot(a, b, preferred_element_type=F32)


def _dot_nt(a, b):
    return lax.dot_general(a, b, (((1,), (1,)), ((), ())), preferred_element_type=F32)


def _dot_tn(a, b):
    return lax.dot_general(a, b, (((0,), (0,)), ((), ())), preferred_element_type=F32)


def _dot_f32(a, b):
    return jnp.dot(a, b, preferred_element_type=F32, precision=lax.Precision.HIGHEST)


def _split(a):
    hi = a.astype(BF16)
    return hi, (a - hi.astype(F32)).astype(BF16)


def _dot_split(a, b):
    return _dot(a[0], b[0]) + (_dot(a[0], b[1]) + _dot(a[1], b[0]))


def _nmm_kernel(n, x_ref, g_ref, *refs):
    h = _rms(x_ref[...], g_ref[...]).astype(BF16)
    for w_ref, o_ref in zip(refs[:n], refs[n:]):
        o_ref[...] = _dot(h, w_ref[...]).astype(o_ref.dtype)


def _norm_matmul(x2d, g, weights, dtypes, tm, name):
    rows, d = x2d.shape
    tm = min(tm, rows)
    n = len(weights)
    in_specs = [pl.BlockSpec((tm, d), lambda i: (i, 0)), pl.BlockSpec((1, d), lambda i: (0, 0))]
    in_specs += [pl.BlockSpec(w.shape, lambda i: (0, 0)) for w in weights]
    out_specs = [pl.BlockSpec((tm, w.shape[1]), lambda i: (i, 0)) for w in weights]
    out_shape = [jax.ShapeDtypeStruct((rows, w.shape[1]), dt) for w, dt in zip(weights, dtypes)]
    return pl.pallas_call(
        functools.partial(_nmm_kernel, n),
        out_shape=out_shape, grid=(rows // tm,), in_specs=in_specs, out_specs=out_specs,
        compiler_params=_cparams(("parallel",)), name=name,
    )(x2d, g.reshape(1, d), *weights)


def _dnprep_kernel(x_ref, halo_ref, cw_ref, sm_ref, al_ref, dt_ref, q_ref, k_ref, v_ref, gb_ref):
    i = pl.program_id(1)
    t = x_ref.shape[1]
    keep = (i > 0).astype(F32)
    for s, o_ref in enumerate((q_ref, k_ref, v_ref)):
        cols = slice(s * DN_WIDTH, (s + 1) * DN_WIDTH)
        xc = jnp.concatenate([halo_ref[0, :, cols] * keep, x_ref[0, :, cols]], axis=0)
        y = jnp.zeros((t, DN_WIDTH), F32)
        for j in range(CONV_K):
            lo = HALO - (CONV_K - 1) + j
            y = y + xc[lo:lo + t] * cw_ref[j:j + 1, cols]
        y = y * _sigmoid(y)
        if s == 2:
            o_ref[0] = y
        else:
            for h in range(DN_HEADS):
                hs = slice(h * DN_DK, (h + 1) * DN_DK)
                seg = y[:, hs]
                seg = seg * lax.rsqrt(jnp.sum(seg * seg, axis=-1, keepdims=True) + NORM_EPS)
                if s == 0:
                    seg = seg * DN_DK ** -0.5
                o_ref[0, :, hs] = seg
    sm = sm_ref[0]
    beta = _sigmoid(sm)
    z = sm + dt_ref[...]
    softplus = jnp.maximum(z, 0.0) + jnp.log1p(jnp.exp(-jnp.abs(z)))
    g = -jnp.exp(al_ref[...]) * softplus
    lane = lax.broadcasted_iota(I32, sm.shape, 1)
    gb_ref[0] = jnp.where(lane < SM_DA, beta, g)


def _dn_prep(qkv, small, conv_w, a_log, dt_bias, t):
    b, s, c = qkv.shape
    t = min(t, s)
    al = jnp.zeros((1, LANES), F32).at[0, SM_DA:SM_DA + DN_HEADS].set(a_log)
    dt = jnp.zeros((1, LANES), F32).at[0, SM_DA:SM_DA + DN_HEADS].set(dt_bias)
    hb = t // HALO
    out_sd = jax.ShapeDtypeStruct((b, s, DN_WIDTH), F32)
    o_spec = pl.BlockSpec((1, t, DN_WIDTH), lambda bi, i: (bi, i, 0))
    return pl.pallas_call(
        _dnprep_kernel,
        out_shape=[out_sd, out_sd, out_sd, jax.ShapeDtypeStruct((b, s, LANES), F32)],
        grid=(b, s // t),
        in_specs=[
            pl.BlockSpec((1, t, c), lambda bi, i: (bi, i, 0)),
            pl.BlockSpec((1, HALO, c), lambda bi, i: (bi, jnp.maximum(i * hb - 1, 0), 0)),
            pl.BlockSpec((CONV_K, c), lambda bi, i: (0, 0)),
            pl.BlockSpec((1, t, LANES), lambda bi, i: (bi, i, 0)),
            pl.BlockSpec((1, LANES), lambda bi, i: (0, 0)),
            pl.BlockSpec((1, LANES), lambda bi, i: (0, 0)),
        ],
        out_specs=[o_spec, o_spec, o_spec, pl.BlockSpec((1, t, LANES), lambda bi, i: (bi, i, 0))],
        compiler_params=_cparams(("parallel", "parallel")), name="dn_prep",
    )(qkv, qkv, conv_w, small, al, dt)


def _dn_kernel(q_ref, k_ref, v_ref, z_ref, gb_ref, nrm_ref, o_ref, s_scr):
    @pl.when(pl.program_id(1) == 0)
    def _():
        s_scr[...] = jnp.zeros_like(s_scr)

    c = CHUNK
    heads = range(DN_HEADS)
    hs = [slice(h * DN_DK, (h + 1) * DN_DK) for h in heads]
    row = lax.broadcasted_iota(I32, (c, c), 0)
    col = lax.broadcasted_iota(I32, (c, c), 1)
    incl = row >= col
    strict = row > col
    tri = incl.astype(F32)
    eye = (row == col).astype(F32)
    gb = gb_ref[0]
    gc_all = _dot_f32(tri, gb)
    gc_t = gc_all.T

    q = [q_ref[0, :, hs[h]] for h in heads]
    k = [k_ref[0, :, hs[h]] for h in heads]
    beta = [gb[:, SM_DB + h:SM_DB + h + 1] for h in heads]
    gcol = [gc_all[:, SM_DA + h:SM_DA + h + 1] for h in heads]
    glast = [gc_all[c - 1:c, SM_DA + h:SM_DA + h + 1] for h in heads]
    decay = [jnp.where(incl, jnp.exp(jnp.where(incl, gcol[h] - gc_t[SM_DA + h:SM_DA + h + 1, :],
                                               0.0)), 0.0) for h in heads]
    kb = [k[h] * beta[h] for h in heads]
    kbf = [k[h].astype(BF16) for h in heads]
    m = [jnp.where(strict, _dot_nt(kb[h].astype(BF16), kbf[h]) * decay[h], 0.0) for h in heads]
    x = [-m[h] for h in heads]
    p = [eye + x[h] for h in heads]
    xs = [_split(x[h]) for h in heads]
    for _ in range(5):
        x = [_dot_split(xs[h], xs[h]) for h in heads]
        xs = [_split(x[h]) for h in heads]
        p = [p[h] + _dot_split(_split(p[h]), xs[h]) for h in heads]
    eg = [jnp.exp(gcol[h]) for h in heads]
    rhs = [jnp.concatenate([v_ref[0, :, hs[h]] * beta[h], kb[h] * eg[h]], axis=1) for h in heads]
    sol = [_dot_split(_split(p[h]), _split(rhs[h])) for h in heads]
    aqk = [(_dot_nt(q[h].astype(BF16), kbf[h]) * decay[h]).astype(BF16) for h in heads]
    qd = [(q[h] * eg[h]).astype(BF16) for h in heads]
    kd = [(k[h] * jnp.exp(glast[h] - gcol[h])).astype(BF16) for h in heads]
    st = [s_scr[h] for h in heads]
    stb = [st[h].astype(BF16) for h in heads]
    vnew = [sol[h][:, :DN_DK] - _dot(sol[h][:, DN_DK:].astype(BF16), stb[h]) for h in heads]
    vnb = [vnew[h].astype(BF16) for h in heads]
    o = [_dot(qd[h], stb[h]) + _dot(aqk[h], vnb[h]) for h in heads]
    for h in heads:
        s_scr[h] = st[h] * jnp.exp(glast[h]) + _dot_tn(kd[h], vnb[h])
    for h in heads:
        zh = z_ref[0, :, hs[h]]
        o_ref[0, :, hs[h]] = (_rms(o[h], nrm_ref[...]) * (zh * _sigmoid(zh))).astype(o_ref.dtype)


def _dn_main(q, k, v, z, gb, dn_norm):
    b, s, _ = q.shape
    blk = pl.BlockSpec((1, CHUNK, DN_WIDTH), lambda bi, i: (bi, i, 0))
    return pl.pallas_call(
        _dn_kernel,
        out_shape=jax.ShapeDtypeStruct((b, s, DN_WIDTH), BF16),
        grid=(b, s // CHUNK),
        in_specs=[blk, blk, blk, blk,
                  pl.BlockSpec((1, CHUNK, LANES), lambda bi, i: (bi, i, 0)),
                  pl.BlockSpec((1, DN_DK), lambda bi, i: (0, 0))],
        out_specs=blk,
        scratch_shapes=[pltpu.VMEM((DN_HEADS, DN_DK, DN_DK), F32)],
        compiler_params=_cparams(("parallel", "arbitrary")), name="dn_main",
    )(q, k, v, z, gb, dn_norm.reshape(1, DN_DK))


def _rope_tab_kernel(pos_ref, f_ref, sg_ref, cs_ref, sn_ref):
    ang = pos_ref[0].astype(F32) * f_ref[...]
    cs_ref[0] = jnp.cos(ang)
    sn_ref[0] = jnp.sin(ang) * sg_ref[...]


def _lane_tables(head_dim):
    rd = head_dim // ROPE_DIV
    half = rd // 2
    inv_freq = ROPE_THETA ** (-(jnp.arange(half, dtype=F32) * 2.0 / rd))
    lane = np.arange(LANES) % head_dim
    sel = np.where(lane < rd, lane % half, 0)
    freq = jnp.where(jnp.asarray(lane < rd), inv_freq[sel], 0.0).reshape(1, LANES)
    sign = np.where(lane < half, -1.0, np.where(lane < rd, 1.0, 0.0)).astype(np.float32)
    return freq.astype(F32), jnp.asarray(sign).reshape(1, LANES)


def _rope_tables(positions, head_dim, t):
    b, s = positions.shape
    t = min(t, s)
    freq, sign = _lane_tables(head_dim)
    sd = jax.ShapeDtypeStruct((b, s, LANES), F32)
    blk = pl.BlockSpec((1, t, LANES), lambda bi, i: (bi, i, 0))
    vec = pl.BlockSpec((1, LANES), lambda bi, i: (0, 0))
    return pl.pallas_call(
        _rope_tab_kernel, out_shape=[sd, sd], grid=(b, s // t),
        in_specs=[pl.BlockSpec((1, t, 1), lambda bi, i: (bi, i, 0)), vec, vec],
        out_specs=[blk, blk],
        compiler_params=_cparams(("parallel", "parallel")), name="rope_tables",
    )(positions.reshape(b, s, 1), freq, sign)


def _rope(x, cs, sn, first_half, half):
    partner = jnp.where(first_half, pltpu.roll(x, LANES - half, 1), pltpu.roll(x, half, 1))
    return x * cs + partner * sn


def _saprep_kernel(qk_ref, ix_ref, sm_ref, cs_ref, sn_ref, ci_ref, si_ref,
                   q_ref, k_ref, iq_ref, ik_ref, iw_ref):
    t = qk_ref.shape[1]
    lane = lax.broadcasted_iota(I32, (t, LANES), 1)
    half_sa = SA_HD // ROPE_DIV // 2
    half_ix = IDX_HD // ROPE_DIV // 2
    fh_sa = lane < half_sa
    fh_ix = (lane % IDX_HD) < half_ix
    cs, sn = cs_ref[0], sn_ref[0]
    ci, si = ci_ref[0], si_ref[0]
    q_scale = SA_HD ** -0.5 * math.log2(math.e)
    for h in range(SA_HEADS):
        hs = slice(h * SA_HD, (h + 1) * SA_HD)
        q_ref[0, :, hs] = (_rope(qk_ref[0, :, hs], cs, sn, fh_sa, half_sa) * q_scale).astype(BF16)
        ks = slice(SA_WIDTH + h * SA_HD, SA_WIDTH + (h + 1) * SA_HD)
        k_ref[0, :, hs] = _rope(qk_ref[0, :, ks], cs, sn, fh_sa, half_sa).astype(BF16)
    for j in range(IDX_HEADS * IDX_HD // LANES):
        r = _rope(ix_ref[0, :, j * LANES:(j + 1) * LANES], ci, si, fh_ix, half_ix).astype(BF16)
        iq_ref[0, 2 * j] = r[:, :IDX_HD]
        iq_ref[0, 2 * j + 1] = r[:, IDX_HD:]
    nq = IDX_HEADS * IDX_HD
    ik_ref[0] = _rope(ix_ref[0, :, nq:nq + LANES], ci, si, fh_ix, half_ix).astype(BF16)
    iw_ref[0] = sm_ref[0] * (IDX_HEADS ** -0.5 * IDX_HD ** -0.5)


def _sa_prep(sqk, idx, small, tabs_sa, tabs_ix, t):
    b, s, _ = sqk.shape
    t = min(t, s)
    tab = pl.BlockSpec((1, t, LANES), lambda bi, i: (bi, i, 0))
    wide = pl.BlockSpec((1, t, SA_WIDTH), lambda bi, i: (bi, i, 0))
    return pl.pallas_call(
        _saprep_kernel,
        out_shape=[jax.ShapeDtypeStruct((b, s, SA_WIDTH), BF16),
                   jax.ShapeDtypeStruct((b, s, SA_WIDTH), BF16),
                   jax.ShapeDtypeStruct((b, IDX_HEADS, s, IDX_HD), BF16),
                   jax.ShapeDtypeStruct((b, s, LANES), BF16),
                   jax.ShapeDtypeStruct((b, s, LANES), F32)],
        grid=(b, s // t),
        in_specs=[pl.BlockSpec((1, t, 2 * SA_WIDTH), lambda bi, i: (bi, i, 0)),
                  pl.BlockSpec((1, t, idx.shape[2]), lambda bi, i: (bi, i, 0)),
                  tab, tab, tab, tab, tab],
        out_specs=[wide, wide,
                   pl.BlockSpec((1, IDX_HEADS, t, IDX_HD), lambda bi, i: (bi, 0, i, 0)),
                   tab, tab],
        compiler_params=_cparams(("parallel", "parallel")), name="sa_prep",
    )(sqk, idx, small, *tabs_sa, *tabs_ix)


def _select_kernel(iq_ref, ikt_ref, iw_ref, bias_ref, key_scr, hi_scr, wb_scr,
                   *, tq, tk, ksel, seq):
    i = pl.program_id(1)
    nkt = ((i + 1) * tq + tk - 1) // tk
    nsub = tk // LANES
    iw = iw_ref[0]
    for h in range(IDX_HEADS):
        wb_scr[h] = jnp.broadcast_to(iw[:, SM_IW + h:SM_IW + h + 1], (tq, LANES))
    q_chunk = (i * tq + lax.broadcasted_iota(I32, (tq, 1), 0)) // CHUNK
    col0 = lax.broadcasted_iota(I32, (1, tk), 1)

    def score_tile(kt, carry):
        off = pl.multiple_of(kt * tk, tk)
        ikt = ikt_ref[0, :, pl.ds(off, tk)]
        for r0 in range(0, tq, SEL_SUB):
            rs = slice(r0, r0 + SEL_SUB)
            acc = [jnp.zeros((SEL_SUB, LANES), F32) for _ in range(nsub)]
            for h in range(IDX_HEADS):
                d = _dot(iq_ref[0, h, rs, :], ikt)
                w = wb_scr[h, rs, :]
                for j in range(nsub):
                    acc[j] = acc[j] + jnp.maximum(d[:, j * LANES:(j + 1) * LANES], 0.0) * w
            adm = ((off + col0) // CHUNK) <= q_chunk[rs]
            sc = jnp.where(adm, jnp.concatenate(acc, axis=1), -jnp.inf)
            sc = jnp.where(sc == 0.0, 0.0, sc)
            bits = pltpu.bitcast(sc, I32)
            key = bits ^ ((bits >> 31) & 0x7FFFFFFF)
            key_scr[rs, pl.ds(off, tk)] = key
            hi_scr[rs, pl.ds(off, tk)] = (key >> 16).astype(I16)
        return carry

    lax.fori_loop(0, nkt, score_tile, 0)

    def count32(pred):
        def body(kt, c):
            off = pl.multiple_of(kt * tk, tk)
            hit = jnp.where(pred(key_scr[:, pl.ds(off, tk)], off), 1.0, 0.0)
            for j in range(nsub):
                c = c + hit[:, j * LANES:(j + 1) * LANES]
            return c
        c = lax.fori_loop(0, nkt, body, jnp.zeros((tq, LANES), F32))
        return jnp.sum(c, axis=1, keepdims=True)

    one16 = jnp.ones((tq, LANES), I16)
    zero16 = jnp.zeros((tq, LANES), I16)

    def count16(scr, pred):
        def body(kt, acc):
            off = kt * tk
            for j in range(nsub):
                blk = scr[:, pl.ds(pl.multiple_of(off + j * LANES, LANES), LANES)]
                acc = acc + jnp.where(pred(blk), one16, zero16)
            return acc
        acc = lax.fori_loop(0, nkt, body, zero16)
        return jnp.sum(acc.astype(I32).astype(F32), axis=1, keepdims=True)

    def lanes16(v):
        return jnp.broadcast_to(v, (tq, LANES)).astype(I16)

    def kth16(scr, target):
        z = lanes16(jnp.zeros((tq, 1), I32))
        t0 = jnp.where(count16(scr, lambda blk: blk >= z) >= target, 0, I16_MIN).astype(I32)

        def step(b, t):
            cand = t | jnp.left_shift(jnp.int32(1), 14 - b)
            cb = lanes16(cand)
            return jnp.where(count16(scr, lambda blk: blk >= cb) >= target, cand, t)

        return lax.fori_loop(0, 15, step, t0)

    kf = float(ksel)
    t_hi = kth16(hi_scr, kf)
    thb = lanes16(t_hi)
    need_lo = kf - count16(hi_scr, lambda blk: blk > thb)

    def build_lo(kt, carry):
        off = pl.multiple_of(kt * tk, tk)
        key = key_scr[:, pl.ds(off, tk)]
        low = (key & 0xFFFF) + I16_MIN
        hi_scr[:, pl.ds(off, tk)] = jnp.where((key >> 16) == t_hi, low, I16_MIN).astype(I16)
        return carry

    lax.fori_loop(0, nkt, build_lo, 0)
    t_lo = kth16(hi_scr, need_lo)
    thr = (t_hi * 65536) | ((t_lo - I16_MIN) & 0xFFFF)

    finite = thr > NEG_INF_KEY
    thr_eff = jnp.maximum(thr, NEG_INF_KEY + 1)
    n_ge = count32(lambda keys, off: keys >= thr_eff)
    tie_rows = jnp.where(finite & (n_ge > kf), 1.0, 0.0)
    has_tie = jnp.max(tie_rows) > 0.0

    bias_ref[...] = jnp.full(bias_ref.shape, NEG_BIAS, bias_ref.dtype)

    @pl.when(jnp.logical_not(has_tie))
    def _():
        def write(kt, carry):
            off = pl.multiple_of(kt * tk, tk)
            sel = key_scr[:, pl.ds(off, tk)] >= thr_eff
            bias_ref[0, :, pl.ds(off, tk)] = jnp.where(sel, 0.0, NEG_BIAS).astype(bias_ref.dtype)
            return carry
        lax.fori_loop(0, nkt, write, 0)

    @pl.when(has_tie)
    def _():
        need = kf - count32(lambda keys, off: keys > thr)
        nbits = max(1, int(np.ceil(np.log2(seq))))

        def idx_step(b, j):
            cand = j | jnp.left_shift(jnp.int32(1), nbits - 1 - b)
            below = count32(lambda keys, off: (keys == thr) & ((off + col0) < cand))
            return jnp.where(below < need, cand, j)

        cut = lax.fori_loop(0, nbits, idx_step, jnp.zeros((tq, 1), I32))

        def write(kt, carry):
            off = pl.multiple_of(kt * tk, tk)
            keys = key_scr[:, pl.ds(off, tk)]
            tie_ok = (keys == thr) & ((off + col0) <= cut)
            sel = (keys >= thr_eff) & ((keys > thr) | tie_ok | jnp.logical_not(finite))
            bias_ref[0, :, pl.ds(off, tk)] = jnp.where(sel, 0.0, NEG_BIAS).astype(bias_ref.dtype)
            return carry
        lax.fori_loop(0, nkt, write, 0)


def _dsa_select(iq, ikt, iw, ksel):
    b, _, s, _ = iq.shape
    tq, tk = min(SEL_TQ, s), min(SEL_TK, s)
    assert s % tk == 0 and tk % tq == 0 and s // LANES < 2 ** 15
    return pl.pallas_call(
        functools.partial(_select_kernel, tq=tq, tk=tk, ksel=ksel, seq=s),
        out_shape=jax.ShapeDtypeStruct((b, s, s), BF16),
        grid=(b, s // tq),
        in_specs=[pl.BlockSpec((1, IDX_HEADS, tq, IDX_HD), lambda bi, i: (bi, 0, i, 0)),
                  pl.BlockSpec((1, IDX_HD, s), lambda bi, i: (bi, 0, 0)),
                  pl.BlockSpec((1, tq, LANES), lambda bi, i: (bi, i, 0))],
        out_specs=pl.BlockSpec((1, tq, s), lambda bi, i: (bi, i, 0)),
        scratch_shapes=[pltpu.VMEM((tq, s), I32), pltpu.VMEM((tq, s), I16),
                        pltpu.VMEM((IDX_HEADS, tq, LANES), F32)],
        compiler_params=_cparams(("parallel", "parallel")), name="dsa_select",
    )(iq, ikt, iw)


def _attn_kernel(it_ref, jt_ref, q_ref, kt_ref, v_ref, b_ref, o_ref, *state, rows):
    m_scr = state[:SA_HEADS]
    acc_scr = state[SA_HEADS:]
    p = pl.program_id(1)
    i = it_ref[p]
    j = jt_ref[p]
    t = q_ref.shape[1]
    tk = kt_ref.shape[3]

    @pl.when(j == 0)
    def _():
        for h in range(SA_HEADS):
            m_scr[h][...] = jnp.full(m_scr[h].shape, NEG_BIAS, F32)
            acc_scr[h][...] = jnp.zeros_like(acc_scr[h])

    ones = jnp.ones((tk, SA_HD), BF16)
    eye = (lax.broadcasted_iota(I32, (rows, rows), 0)
           == lax.broadcasted_iota(I32, (rows, rows), 1)).astype(BF16)

    def update(r, carry):
        rs = pl.ds(pl.multiple_of(r * rows, rows), rows)
        heads = range(SA_HEADS)
        hs = [slice(h * SA_HD, (h + 1) * SA_HD) for h in heads]
        mask = b_ref[0, rs, :]
        s = [_dot(jnp.concatenate([q_ref[0, rs, hs[h]], eye], axis=1),
                  jnp.concatenate([kt_ref[0, h], mask], axis=0)) for h in heads]
        m_old = [m_scr[h][rs, :] for h in heads]
        m_new = [jnp.maximum(m_old[h], jnp.max(s[h], axis=1, keepdims=True)) for h in heads]
        pr = [jnp.exp2(s[h] - jnp.concatenate([m_new[h]] * (tk // LANES), axis=1)).astype(BF16)
              for h in heads]
        pv = [_dot(pr[h], jnp.concatenate([v_ref[0, :, hs[h]], ones], axis=1)) for h in heads]
        for h in heads:
            alpha = jnp.exp2(m_old[h] - m_new[h])
            acc_scr[h][rs, :] = acc_scr[h][rs, :] * jnp.concatenate([alpha, alpha], axis=1) + pv[h]
            m_scr[h][rs, :] = m_new[h]
        return carry

    lax.fori_loop(0, t // rows, update, 0)

    @pl.when(j == i)
    def _():
        for h in range(SA_HEADS):
            hs = slice(h * SA_HD, (h + 1) * SA_HD)
            o_ref[0, :, hs] = (acc_scr[h][:, :SA_HD] / acc_scr[h][:, SA_HD:]).astype(o_ref.dtype)


def _dsa_attend(q, k, v, bias):
    b, s, w = q.shape
    t = min(ATT_T, s)
    n = s // t
    pairs = [(i, j) for i in range(n) for j in range(i + 1)]
    it = jnp.asarray([p[0] for p in pairs], I32)
    jt = jnp.asarray([p[1] for p in pairs], I32)
    grid_spec = pltpu.PrefetchScalarGridSpec(
        num_scalar_prefetch=2, grid=(b, len(pairs)),
        in_specs=[pl.BlockSpec((1, t, w), lambda bi, p, it, jt: (bi, it[p], 0)),
                  pl.BlockSpec((1, SA_HEADS, SA_HD, t), lambda bi, p, it, jt: (bi, 0, 0, jt[p])),
                  pl.BlockSpec((1, t, w), lambda bi, p, it, jt: (bi, jt[p], 0)),
                  pl.BlockSpec((1, t, t), lambda bi, p, it, jt: (bi, it[p], jt[p]))],
        out_specs=pl.BlockSpec((1, t, w), lambda bi, p, it, jt: (bi, it[p], 0)),
        scratch_shapes=([pltpu.VMEM((t, LANES), F32) for _ in range(SA_HEADS)]
                        + [pltpu.VMEM((t, 2 * SA_HD), F32) for _ in range(SA_HEADS)]))
    return pl.pallas_call(
        functools.partial(_attn_kernel, rows=min(ATT_ROWS, t)),
        out_shape=jax.ShapeDtypeStruct((b, s, w), BF16), grid_spec=grid_spec,
        compiler_params=_cparams(("parallel", "arbitrary")), name="dsa_attend",
    )(it, jt, q, k, v, bias)


def _xattn_kernel(q_ref, mk_ref, mv_ref, o_ref):
    for h in range(XA_HEADS):
        hs = slice(h * XA_HD, (h + 1) * XA_HD)
        s = _dot_nt(q_ref[0, :, hs], mk_ref[0, :, hs]) * XA_HD ** -0.5
        p = jnp.exp(s - jnp.max(s, axis=1, keepdims=True))
        o = _dot(p.astype(BF16), mv_ref[0, :, hs]) / jnp.sum(p, axis=1, keepdims=True)
        o_ref[0, :, hs] = o.astype(o_ref.dtype)


def _mem_attend(xq, mk, mv, t):
    b, s, w = xq.shape
    t = min(t, s)
    m = mk.shape[1]
    return pl.pallas_call(
        _xattn_kernel,
        out_shape=jax.ShapeDtypeStruct((b, s, w), BF16),
        grid=(b, s // t),
        in_specs=[pl.BlockSpec((1, t, w), lambda bi, i: (bi, i, 0)),
                  pl.BlockSpec((1, m, w), lambda bi, i: (bi, 0, 0)),
                  pl.BlockSpec((1, m, w), lambda bi, i: (bi, 0, 0))],
        out_specs=pl.BlockSpec((1, t, w), lambda bi, i: (bi, i, 0)),
        compiler_params=_cparams(("parallel", "parallel")), name="mem_attend",
    )(xq, mk, mv)


def _merge_kernel(x_ref, od_ref, os_ref, ox_ref, gl_ref, bg_ref, wb_ref, wo_ref, o_ref):
    merged = jnp.zeros(x_ref.shape, F32)
    for r, br_ref in enumerate((od_ref, os_ref, ox_ref)):
        cs = slice(r * D_MODEL, (r + 1) * D_MODEL)
        gate = _sigmoid(gl_ref[:, cs] + bg_ref[:, cs])
        merged = merged + gate * _dot(br_ref[...], wb_ref[r])
    o_ref[...] = x_ref[...] + _dot(merged.astype(BF16), wo_ref[...])


def _merge(x2d, o_dn, o_sa, o_xa, gl, b_gate, w_branch, w_out, tm):
    rows, d = x2d.shape
    tm = min(tm, rows)
    row = pl.BlockSpec((tm, d), lambda i: (i, 0))
    return pl.pallas_call(
        _merge_kernel,
        out_shape=jax.ShapeDtypeStruct((rows, d), F32),
        grid=(rows // tm,),
        in_specs=[row, row, row, row,
                  pl.BlockSpec((tm, N_BRANCH * d), lambda i: (i, 0)),
                  pl.BlockSpec((1, N_BRANCH * d), lambda i: (0, 0)),
                  pl.BlockSpec((N_BRANCH, d, d), lambda i: (0, 0, 0)),
                  pl.BlockSpec((d, d), lambda i: (0, 0))],
        out_specs=row,
        compiler_params=_cparams(("parallel",)), name="merge",
    )(x2d, o_dn, o_sa, o_xa, gl, b_gate.reshape(1, N_BRANCH * d), w_branch, w_out)


def _ffn_kernel(x_ref, g_ref, wg_ref, wu_ref, wo_ref, o_ref, h_scr, acc_scr):
    f = pl.program_id(1)

    @pl.when(f == 0)
    def _():
        h_scr[...] = _rms(x_ref[...], g_ref[...]).astype(BF16)
        acc_scr[...] = x_ref[...]

    h = h_scr[...]
    gt = _dot(h, wg_ref[...])
    up = _dot(h, wu_ref[...])
    act = (gt * _sigmoid(gt) * up).astype(BF16)
    acc_scr[...] += _dot(act, wo_ref[...])

    @pl.when(f == pl.num_programs(1) - 1)
    def _():
        o_ref[...] = acc_scr[...]


def _ffn(x2d, g, wg, wu, wo, tm, tf):
    rows, d = x2d.shape
    tm = min(tm, rows)
    nf = wg.shape[1] // tf
    return pl.pallas_call(
        _ffn_kernel,
        out_shape=jax.ShapeDtypeStruct((rows, d), F32),
        grid=(rows // tm, nf),
        in_specs=[pl.BlockSpec((tm, d), lambda i, f: (i, 0)),
                  pl.BlockSpec((1, d), lambda i, f: (0, 0)),
                  pl.BlockSpec((d, tf), lambda i, f: (0, f)),
                  pl.BlockSpec((d, tf), lambda i, f: (0, f)),
                  pl.BlockSpec((tf, d), lambda i, f: (f, 0))],
        out_specs=pl.BlockSpec((tm, d), lambda i, f: (i, 0)),
        scratch_shapes=[pltpu.VMEM((tm, d), BF16), pltpu.VMEM((tm, d), F32)],
        compiler_params=_cparams(("parallel", "arbitrary")), name="ffn",
    )(x2d, g.reshape(1, d), wg, wu, wo)


def _final_norm_kernel(x_ref, g_ref, o_ref):
    o_ref[...] = _rms(x_ref[...], g_ref[...])


def _final_norm(x2d, g, tm):
    rows, d = x2d.shape
    tm = min(tm, rows)
    row = pl.BlockSpec((tm, d), lambda i: (i, 0))
    return pl.pallas_call(
        _final_norm_kernel, out_shape=jax.ShapeDtypeStruct((rows, d), F32), grid=(rows // tm,),
        in_specs=[row, pl.BlockSpec((1, d), lambda i: (0, 0))], out_specs=row,
        compiler_params=_cparams(("parallel",)), name="final_norm",
    )(x2d, g.reshape(1, d))


def _pad_cols(w, width):
    return jnp.pad(w, ((0, 0), (0, width - w.shape[1])))


def _layer(x2d, b, s, mem2d, tabs_sa, tabs_ix, norm_mix, norm_mem, w_in, b_gate, conv_w, a_log,
           dt_bias, dn_norm, w_mem_kv, w_branch, w_out, norm_ffn, w_ffn_in, w_ffn_out):
    wb = w_in.astype(BF16)
    w_small = _pad_cols(jnp.concatenate(
        [wb[:, OFF_DB:OFF_DB + DN_HEADS], wb[:, OFF_DA:OFF_DA + DN_HEADS],
         wb[:, OFF_IW:OFF_IW + IDX_HEADS]], axis=1), LANES)
    w_idx = _pad_cols(wb[:, OFF_IQ:OFF_IW], (IDX_HEADS + 2) * IDX_HD)

    dqkv, dz, small = _norm_matmul(
        x2d, norm_mix, [wb[:, OFF_DQKV:OFF_DZ], wb[:, OFF_DZ:OFF_DB], w_small],
        [F32, F32, F32], 256, "in_proj_dn")
    sqk, sv, idx = _norm_matmul(
        x2d, norm_mix, [wb[:, OFF_SQK:OFF_SV], wb[:, OFF_SV:OFF_IQ], w_idx],
        [F32, BF16, F32], 256, "in_proj_sa")
    xq, gl = _norm_matmul(
        x2d, norm_mix, [wb[:, OFF_XQ:OFF_GL], wb[:, OFF_GL:N_IN]], [BF16, F32], 256, "in_proj_xg")

    def r3(a):
        return a.reshape(b, s, a.shape[-1])

    dq, dk, dv, gb = _dn_prep(r3(dqkv), r3(small), conv_w, a_log, dt_bias, 256)
    o_dn = _dn_main(dq, dk, dv, r3(dz), gb, dn_norm)

    sq, sk, iq, ik, iw = _sa_prep(r3(sqk), r3(idx), r3(small), tabs_sa, tabs_ix, 256)
    ikt = jnp.swapaxes(ik[:, :, :IDX_HD], 1, 2)
    bias = _dsa_select(iq, ikt, iw, min(TOPK_MAX, s // 4))
    skt = jnp.transpose(sk.reshape(b, s, SA_HEADS, SA_HD), (0, 2, 3, 1))
    o_sa = _dsa_attend(sq, skt, r3(sv), bias)

    wkv = w_mem_kv.astype(BF16)
    mk, mv = _norm_matmul(mem2d, norm_mem, [wkv[:, :XA_WIDTH], wkv[:, XA_WIDTH:]],
                          [BF16, BF16], 256, "mem_kv")
    n_mem = mem2d.shape[0] // b
    o_xa = _mem_attend(r3(xq), mk.reshape(b, n_mem, XA_WIDTH), mv.reshape(b, n_mem, XA_WIDTH), 512)

    x2d = _merge(x2d, o_dn.reshape(b * s, -1), o_sa.reshape(b * s, -1), o_xa.reshape(b * s, -1),
                 gl, b_gate, w_branch.astype(BF16), w_out.astype(BF16), 256)
    wf = w_ffn_in.astype(BF16)
    return _ffn(x2d, norm_ffn, wf[:, :D_FF], wf[:, D_FF:], w_ffn_out.astype(BF16), 512, D_FF // 2)


def kernel(x, mem, positions, norm_mix, norm_mem, w_in, b_gate, conv_w, a_log, dt_bias, dn_norm,
           w_mem_kv, w_branch, w_out, norm_ffn, w_ffn_in, w_ffn_out, norm_final):
    b, s, d = x.shape
    assert d == D_MODEL and s % ATT_T == 0
    x2d = x.reshape(b * s, d)
    mem2d = mem.reshape(-1, d)
    tabs_sa = _rope_tables(positions, SA_HD, 512)
    tabs_ix = _rope_tables(positions, IDX_HD, 512)
    for l in range(w_in.shape[0]):
        x2d = _layer(x2d, b, s, mem2d, tabs_sa, tabs_ix, norm_mix[l], norm_mem[l], w_in[l],
                     b_gate[l], conv_w[l], a_log[l], dt_bias[l], dn_norm[l], w_mem_kv[l],
                     w_branch[l], w_out[l], norm_ffn[l], w_ffn_in[l], w_ffn_out[l])
    return _final_norm(x2d, norm_final, 512).reshape(b, s, d)
```

```python
import functools
import math

import jax
import jax.numpy as jnp
import numpy as np
from jax import lax
from jax.experimental import pallas as pl
from jax.experimental.pallas import tpu as pltpu

F32 = jnp.float32
BF16 = jnp.bfloat16
I16 = jnp.int16
I32 = jnp.int32

D_MODEL = 1024
CHUNK = 64
DN_HEADS = 8
DN_DK = 128
DN_WIDTH = DN_HEADS * DN_DK
CONV_K = 4
SA_HEADS = 8
SA_HD = 128
SA_WIDTH = SA_HEADS * SA_HD
IDX_HEADS = 8
IDX_HD = 64
TOPK_MAX = 256
XA_HEADS = 4
XA_HD = 256
XA_WIDTH = XA_HEADS * XA_HD
ROPE_THETA = 500000.0
ROPE_DIV = 4
D_FF = 2816
N_BRANCH = 3
NORM_EPS = 1e-6

LANES = 128
HALO = 8
VMEM_LIMIT = 48 * 1024 * 1024

OFF_DQKV = 0
OFF_DZ = 3 * DN_WIDTH
OFF_DB = OFF_DZ + DN_WIDTH
OFF_DA = OFF_DB + DN_HEADS
OFF_SQK = OFF_DA + DN_HEADS
OFF_SV = OFF_SQK + 2 * SA_WIDTH
OFF_IQ = OFF_SV + SA_WIDTH
OFF_IK = OFF_IQ + IDX_HEADS * IDX_HD
OFF_IW = OFF_IK + IDX_HD
OFF_XQ = OFF_IW + IDX_HEADS
OFF_GL = OFF_XQ + XA_WIDTH
N_IN = OFF_GL + N_BRANCH * D_MODEL

SM_DB = 0
SM_DA = DN_HEADS
SM_IW = 2 * DN_HEADS

NEG_BIAS = -1e30
INT_MIN = -2 ** 31
NEG_INF_KEY = -2 ** 31 + 0x7FFFFF
I16_MIN = -2 ** 15

SEL_TQ = 128
SEL_TK = 1024
ATT_T = 512
ATT_ROWS = 128


def _cparams(sem):
    return pltpu.CompilerParams(dimension_semantics=sem, vmem_limit_bytes=VMEM_LIMIT)


def _rms(x, g):
    return x * lax.rsqrt(jnp.mean(x * x, axis=-1, keepdims=True) + NORM_EPS) * g


def _sigmoid(x):
    return 1.0 / (1.0 + jnp.exp(-x))


def _dot(a, b):
    return jnp.dot(a, b, preferred_element_type=F32)


def _dot_nt(a, b):
    return lax.dot_general(a, b, (((1,), (1,)), ((), ())), preferred_element_type=F32)


def _dot_tn(a, b):
    return lax.dot_general(a, b, (((0,), (0,)), ((), ())), preferred_element_type=F32)


def _dot_f32(a, b):
    return jnp.dot(a, b, preferred_element_type=F32, precision=lax.Precision.HIGHEST)


def _split(a):
    hi = a.astype(BF16)
    return hi, (a - hi.astype(F32)).astype(BF16)


def _dot_split(a, b):
    return _dot(a[0], b[0]) + (_dot(a[0], b[1]) + _dot(a[1], b[0]))


def _nmm_kernel(n, x_ref, g_ref, *refs):
    h = _rms(x_ref[...], g_ref[...]).astype(BF16)
    for w_ref, o_ref in zip(refs[:n], refs[n:]):
        o_ref[...] = _dot(h, w_ref[...]).astype(o_ref.dtype)


def _norm_matmul(x2d, g, weights, dtypes, tm, name):
    rows, d = x2d.shape
    tm = min(tm, rows)
    n = len(weights)
    in_specs = [pl.BlockSpec((tm, d), lambda i: (i, 0)), pl.BlockSpec((1, d), lambda i: (0, 0))]
    in_specs += [pl.BlockSpec(w.shape, lambda i: (0, 0)) for w in weights]
    out_specs = [pl.BlockSpec((tm, w.shape[1]), lambda i: (i, 0)) for w in weights]
    out_shape = [jax.ShapeDtypeStruct((rows, w.shape[1]), dt) for w, dt in zip(weights, dtypes)]
    return pl.pallas_call(
        functools.partial(_nmm_kernel, n),
        out_shape=out_shape, grid=(rows // tm,), in_specs=in_specs, out_specs=out_specs,
        compiler_params=_cparams(("parallel",)), name=name,
    )(x2d, g.reshape(1, d), *weights)


def _dnprep_kernel(x_ref, halo_ref, cw_ref, sm_ref, al_ref, dt_ref, q_ref, k_ref, v_ref, gb_ref):
    i = pl.program_id(1)
    t = x_ref.shape[1]
    keep = (i > 0).astype(F32)
    for s, o_ref in enumerate((q_ref, k_ref, v_ref)):
        cols = slice(s * DN_WIDTH, (s + 1) * DN_WIDTH)
        xc = jnp.concatenate([halo_ref[0, :, cols] * keep, x_ref[0, :, cols]], axis=0)
        y = jnp.zeros((t, DN_WIDTH), F32)
        for j in range(CONV_K):
            lo = HALO - (CONV_K - 1) + j
            y = y + xc[lo:lo + t] * cw_ref[j:j + 1, cols]
        y = y * _sigmoid(y)
        if s == 2:
            o_ref[0] = y
        else:
            for h in range(DN_HEADS):
                hs = slice(h * DN_DK, (h + 1) * DN_DK)
                seg = y[:, hs]
                seg = seg * lax.rsqrt(jnp.sum(seg * seg, axis=-1, keepdims=True) + NORM_EPS)
                if s == 0:
                    seg = seg * DN_DK ** -0.5
                o_ref[0, :, hs] = seg
    sm = sm_ref[0]
    beta = _sigmoid(sm)
    z = sm + dt_ref[...]
    softplus = jnp.maximum(z, 0.0) + jnp.log1p(jnp.exp(-jnp.abs(z)))
    g = -jnp.exp(al_ref[...]) * softplus
    lane = lax.broadcasted_iota(I32, sm.shape, 1)
    gb_ref[0] = jnp.where(lane < SM_DA, beta, g)


def _dn_prep(qkv, small, conv_w, a_log, dt_bias, t):
    b, s, c = qkv.shape
    t = min(t, s)
    al = jnp.zeros((1, LANES), F32).at[0, SM_DA:SM_DA + DN_HEADS].set(a_log)
    dt = jnp.zeros((1, LANES), F32).at[0, SM_DA:SM_DA + DN_HEADS].set(dt_bias)
    hb = t // HALO
    out_sd = jax.ShapeDtypeStruct((b, s, DN_WIDTH), F32)
    o_spec = pl.BlockSpec((1, t, DN_WIDTH), lambda bi, i: (bi, i, 0))
    return pl.pallas_call(
        _dnprep_kernel,
        out_shape=[out_sd, out_sd, out_sd, jax.ShapeDtypeStruct((b, s, LANES), F32)],
        grid=(b, s // t),
        in_specs=[
            pl.BlockSpec((1, t, c), lambda bi, i: (bi, i, 0)),
            pl.BlockSpec((1, HALO, c), lambda bi, i: (bi, jnp.maximum(i * hb - 1, 0), 0)),
            pl.BlockSpec((CONV_K, c), lambda bi, i: (0, 0)),
            pl.BlockSpec((1, t, LANES), lambda bi, i: (bi, i, 0)),
            pl.BlockSpec((1, LANES), lambda bi, i: (0, 0)),
            pl.BlockSpec((1, LANES), lambda bi, i: (0, 0)),
        ],
        out_specs=[o_spec, o_spec, o_spec, pl.BlockSpec((1, t, LANES), lambda bi, i: (bi, i, 0))],
        compiler_params=_cparams(("parallel", "parallel")), name="dn_prep",
    )(qkv, qkv, conv_w, small, al, dt)


def _dn_kernel(q_ref, k_ref, v_ref, z_ref, gb_ref, nrm_ref, o_ref, s_scr):
    @pl.when(pl.program_id(1) == 0)
    def _():
        s_scr[...] = jnp.zeros_like(s_scr)

    c = CHUNK
    heads = range(DN_HEADS)
    hs = [slice(h * DN_DK, (h + 1) * DN_DK) for h in heads]
    row = lax.broadcasted_iota(I32, (c, c), 0)
    col = lax.broadcasted_iota(I32, (c, c), 1)
    incl = row >= col
    strict = row > col
    tri = incl.astype(F32)
    eye = (row == col).astype(F32)
    gb = gb_ref[0]
    gc_all = _dot_f32(tri, gb)
    gc_t = gc_all.T

    q = [q_ref[0, :, hs[h]] for h in heads]
    k = [k_ref[0, :, hs[h]] for h in heads]
    beta = [gb[:, SM_DB + h:SM_DB + h + 1] for h in heads]
    gcol = [gc_all[:, SM_DA + h:SM_DA + h + 1] for h in heads]
    glast = [gc_all[c - 1:c, SM_DA + h:SM_DA + h + 1] for h in heads]
    decay = [jnp.where(incl, jnp.exp(jnp.where(incl, gcol[h] - gc_t[SM_DA + h:SM_DA + h + 1, :],
                                               0.0)), 0.0) for h in heads]
    kb = [k[h] * beta[h] for h in heads]
    kbf = [k[h].astype(BF16) for h in heads]
    m = [jnp.where(strict, _dot_nt(kb[h].astype(BF16), kbf[h]) * decay[h], 0.0) for h in heads]
    x = [-m[h] for h in heads]
    p = [eye + x[h] for h in heads]
    xs = [_split(x[h]) for h in heads]
    for _ in range(5):
        x = [_dot_split(xs[h], xs[h]) for h in heads]
        xs = [_split(x[h]) for h in heads]
        p = [p[h] + _dot_split(_split(p[h]), xs[h]) for h in heads]
    eg = [jnp.exp(gcol[h]) for h in heads]
    rhs = [jnp.concatenate([v_ref[0, :, hs[h]] * beta[h], kb[h] * eg[h]], axis=1) for h in heads]
    sol = [_dot_split(_split(p[h]), _split(rhs[h])) for h in heads]
    aqk = [(_dot_nt(q[h].astype(BF16), kbf[h]) * decay[h]).astype(BF16) for h in heads]
    qd = [(q[h] * eg[h]).astype(BF16) for h in heads]
    kd = [(k[h] * jnp.exp(glast[h] - gcol[h])).astype(BF16) for h in heads]
    st = [s_scr[h] for h in heads]
    stb = [st[h].astype(BF16) for h in heads]
    vnew = [sol[h][:, :DN_DK] - _dot(sol[h][:, DN_DK:].astype(BF16), stb[h]) for h in heads]
    vnb = [vnew[h].astype(BF16) for h in heads]
    o = [_dot(qd[h], stb[h]) + _dot(aqk[h], vnb[h]) for h in heads]
    for h in heads:
        s_scr[h] = st[h] * jnp.exp(glast[h]) + _dot_tn(kd[h], vnb[h])
    for h in heads:
        zh = z_ref[0, :, hs[h]]
        o_ref[0, :, hs[h]] = (_rms(o[h], nrm_ref[...]) * (zh * _sigmoid(zh))).astype(o_ref.dtype)


def _dn_main(q, k, v, z, gb, dn_norm):
    b, s, _ = q.shape
    blk = pl.BlockSpec((1, CHUNK, DN_WIDTH), lambda bi, i: (bi, i, 0))
    return pl.pallas_call(
        _dn_kernel,
        out_shape=jax.ShapeDtypeStruct((b, s, DN_WIDTH), BF16),
        grid=(b, s // CHUNK),
        in_specs=[blk, blk, blk, blk,
                  pl.BlockSpec((1, CHUNK, LANES), lambda bi, i: (bi, i, 0)),
                  pl.BlockSpec((1, DN_DK), lambda bi, i: (0, 0))],
        out_specs=blk,
        scratch_shapes=[pltpu.VMEM((DN_HEADS, DN_DK, DN_DK), F32)],
        compiler_params=_cparams(("parallel", "arbitrary")), name="dn_main",
    )(q, k, v, z, gb, dn_norm.reshape(1, DN_DK))


def _rope_tab_kernel(pos_ref, f_ref, sg_ref, cs_ref, sn_ref):
    ang = pos_ref[0].astype(F32) * f_ref[...]
    cs_ref[0] = jnp.cos(ang)
    sn_ref[0] = jnp.sin(ang) * sg_ref[...]


def _lane_tables(head_dim):
    rd = head_dim // ROPE_DIV
    half = rd // 2
    inv_freq = ROPE_THETA ** (-(jnp.arange(half, dtype=F32) * 2.0 / rd))
    lane = np.arange(LANES) % head_dim
    sel = np.where(lane < rd, lane % half, 0)
    freq = jnp.where(jnp.asarray(lane < rd), inv_freq[sel], 0.0).reshape(1, LANES)
    sign = np.where(lane < half, -1.0, np.where(lane < rd, 1.0, 0.0)).astype(np.float32)
    return freq.astype(F32), jnp.asarray(sign).reshape(1, LANES)


def _rope_tables(positions, head_dim, t):
    b, s = positions.shape
    t = min(t, s)
    freq, sign = _lane_tables(head_dim)
    sd = jax.ShapeDtypeStruct((b, s, LANES), F32)
    blk = pl.BlockSpec((1, t, LANES), lambda bi, i: (bi, i, 0))
    vec = pl.BlockSpec((1, LANES), lambda bi, i: (0, 0))
    return pl.pallas_call(
        _rope_tab_kernel, out_shape=[sd, sd], grid=(b, s // t),
        in_specs=[pl.BlockSpec((1, t, 1), lambda bi, i: (bi, i, 0)), vec, vec],
        out_specs=[blk, blk],
        compiler_params=_cparams(("parallel", "parallel")), name="rope_tables",
    )(positions.reshape(b, s, 1), freq, sign)


def _rope(x, cs, sn, first_half, half):
    partner = jnp.where(first_half, pltpu.roll(x, LANES - half, 1), pltpu.roll(x, half, 1))
    return x * cs + partner * sn


def _saprep_kernel(qk_ref, ix_ref, sm_ref, cs_ref, sn_ref, ci_ref, si_ref,
                   q_ref, k_ref, iq_ref, ik_ref, iw_ref):
    t = qk_ref.shape[1]
    lane = lax.broadcasted_iota(I32, (t, LANES), 1)
    half_sa = SA_HD // ROPE_DIV // 2
    half_ix = IDX_HD // ROPE_DIV // 2
    fh_sa = lane < half_sa
    fh_ix = (lane % IDX_HD) < half_ix
    cs, sn = cs_ref[0], sn_ref[0]
    ci, si = ci_ref[0], si_ref[0]
    q_scale = SA_HD ** -0.5 * math.log2(math.e)
    for h in range(SA_HEADS):
        hs = slice(h * SA_HD, (h + 1) * SA_HD)
        q_ref[0, :, hs] = (_rope(qk_ref[0, :, hs], cs, sn, fh_sa, half_sa) * q_scale).astype(BF16)
        ks = slice(SA_WIDTH + h * SA_HD, SA_WIDTH + (h + 1) * SA_HD)
        k_ref[0, :, hs] = _rope(qk_ref[0, :, ks], cs, sn, fh_sa, half_sa).astype(BF16)
    for j in range(IDX_HEADS * IDX_HD // LANES):
        r = _rope(ix_ref[0, :, j * LANES:(j + 1) * LANES], ci, si, fh_ix, half_ix).astype(BF16)
        iq_ref[0, 2 * j] = r[:, :IDX_HD]
        iq_ref[0, 2 * j + 1] = r[:, IDX_HD:]
    nq = IDX_HEADS * IDX_HD
    ik_ref[0] = _rope(ix_ref[0, :, nq:nq + LANES], ci, si, fh_ix, half_ix).astype(BF16)
    iw_ref[0] = sm_ref[0] * (IDX_HEADS ** -0.5 * IDX_HD ** -0.5)


def _sa_prep(sqk, idx, small, tabs_sa, tabs_ix, t):
    b, s, _ = sqk.shape
    t = min(t, s)
    tab = pl.BlockSpec((1, t, LANES), lambda bi, i: (bi, i, 0))
    wide = pl.BlockSpec((1, t, SA_WIDTH), lambda bi, i: (bi, i, 0))
    return pl.pallas_call(
        _saprep_kernel,
        out_shape=[jax.ShapeDtypeStruct((b, s, SA_WIDTH), BF16),
                   jax.ShapeDtypeStruct((b, s, SA_WIDTH), BF16),
                   jax.ShapeDtypeStruct((b, IDX_HEADS, s, IDX_HD), BF16),
                   jax.ShapeDtypeStruct((b, s, LANES), BF16),
                   jax.ShapeDtypeStruct((b, s, LANES), F32)],
        grid=(b, s // t),
        in_specs=[pl.BlockSpec((1, t, 2 * SA_WIDTH), lambda bi, i: (bi, i, 0)),
                  pl.BlockSpec((1, t, idx.shape[2]), lambda bi, i: (bi, i, 0)),
                  tab, tab, tab, tab, tab],
        out_specs=[wide, wide,
                   pl.BlockSpec((1, IDX_HEADS, t, IDX_HD), lambda bi, i: (bi, 0, i, 0)),
                   tab, tab],
        compiler_params=_cparams(("parallel", "parallel")), name="sa_prep",
    )(sqk, idx, small, *tabs_sa, *tabs_ix)


def _select_kernel(iq_ref, ikt_ref, iw_ref, bias_ref, key_scr, hi_scr, lo_scr, wb_scr,
                   *, tq, tk, ksel, seq):
    i = pl.program_id(1)
    nkt = ((i + 1) * tq + tk - 1) // tk
    nsub = tk // LANES
    iw = iw_ref[0]
    for h in range(IDX_HEADS):
        wb_scr[h] = jnp.broadcast_to(iw[:, SM_IW + h:SM_IW + h + 1], (tq, LANES))
    q_chunk = (i * tq + lax.broadcasted_iota(I32, (tq, 1), 0)) // CHUNK
    col0 = lax.broadcasted_iota(I32, (1, tk), 1)

    def score_tile(kt, carry):
        off = pl.multiple_of(kt * tk, tk)
        ikt = ikt_ref[0, :, pl.ds(off, tk)]
        acc = [jnp.zeros((tq, LANES), F32) for _ in range(nsub)]
        for h in range(IDX_HEADS):
            d = _dot(iq_ref[0, h], ikt)
            w = wb_scr[h]
            for j in range(nsub):
                acc[j] = acc[j] + jnp.maximum(d[:, j * LANES:(j + 1) * LANES], 0.0) * w
        adm = ((off + col0) // CHUNK) <= q_chunk
        sc = jnp.where(adm, jnp.concatenate(acc, axis=1), -jnp.inf)
        sc = jnp.where(sc == 0.0, 0.0, sc)
        bits = pltpu.bitcast(sc, I32)
        key = bits ^ ((bits >> 31) & 0x7FFFFFFF)
        key_scr[:, pl.ds(off, tk)] = key
        hi_scr[:, pl.ds(off, tk)] = (key >> 16).astype(I16)
        return carry

    lax.fori_loop(0, nkt, score_tile, 0)

    def count32(pred):
        def body(kt, c):
            off = pl.multiple_of(kt * tk, tk)
            hit = jnp.where(pred(key_scr[:, pl.ds(off, tk)], off), 1.0, 0.0)
            for j in range(nsub):
                c = c + hit[:, j * LANES:(j + 1) * LANES]
            return c
        c = lax.fori_loop(0, nkt, body, jnp.zeros((tq, LANES), F32))
        return jnp.sum(c, axis=1, keepdims=True)

    one16 = jnp.ones((tq, LANES), I16)
    zero16 = jnp.zeros((tq, LANES), I16)

    def count16(scr, pred):
        def body(kt, acc):
            off = kt * tk
            for j in range(nsub):
                blk = scr[:, pl.ds(pl.multiple_of(off + j * LANES, LANES), LANES)]
                acc = acc + jnp.where(pred(blk), one16, zero16)
            return acc
        acc = lax.fori_loop(0, nkt, body, zero16)
        return jnp.sum(acc.astype(I32).astype(F32), axis=1, keepdims=True)

    def lanes16(v):
        return jnp.broadcast_to(v, (tq, LANES)).astype(I16)

    def kth16(scr, target):
        z = lanes16(jnp.zeros((tq, 1), I32))
        t0 = jnp.where(count16(scr, lambda blk: blk >= z) >= target, 0, I16_MIN).astype(I32)

        def step(b, t):
            cand = t | jnp.left_shift(jnp.int32(1), 14 - b)
            cb = lanes16(cand)
            return jnp.where(count16(scr, lambda blk: blk >= cb) >= target, cand, t)

        return lax.fori_loop(0, 15, step, t0)

    kf = float(ksel)
    t_hi = kth16(hi_scr, kf)
    thb = lanes16(t_hi)
    need_lo = kf - count16(hi_scr, lambda blk: blk > thb)

    def build_lo(kt, carry):
        off = pl.multiple_of(kt * tk, tk)
        key = key_scr[:, pl.ds(off, tk)]
        low = (key & 0xFFFF) + I16_MIN
        lo_scr[:, pl.ds(off, tk)] = jnp.where((key >> 16) == t_hi, low, I16_MIN).astype(I16)
        return carry

    lax.fori_loop(0, nkt, build_lo, 0)
    t_lo = kth16(lo_scr, need_lo)
    thr = (t_hi * 65536) | ((t_lo - I16_MIN) & 0xFFFF)

    finite = thr > NEG_INF_KEY
    thr_eff = jnp.maximum(thr, NEG_INF_KEY + 1)
    n_ge = count32(lambda keys, off: keys >= thr_eff)
    tie_rows = jnp.where(finite & (n_ge > kf), 1.0, 0.0)
    has_tie = jnp.max(tie_rows) > 0.0

    bias_ref[...] = jnp.full(bias_ref.shape, NEG_BIAS, bias_ref.dtype)

    @pl.when(jnp.logical_not(has_tie))
    def _():
        def write(kt, carry):
            off = pl.multiple_of(kt * tk, tk)
            sel = key_scr[:, pl.ds(off, tk)] >= thr_eff
            bias_ref[0, :, pl.ds(off, tk)] = jnp.where(sel, 0.0, NEG_BIAS).astype(bias_ref.dtype)
            return carry
        lax.fori_loop(0, nkt, write, 0)

    @pl.when(has_tie)
    def _():
        need = kf - count32(lambda keys, off: keys > thr)
        nbits = max(1, int(np.ceil(np.log2(seq))))

        def idx_step(b, j):
            cand = j | jnp.left_shift(jnp.int32(1), nbits - 1 - b)
            below = count32(lambda keys, off: (keys == thr) & ((off + col0) < cand))
            return jnp.where(below < need, cand, j)

        cut = lax.fori_loop(0, nbits, idx_step, jnp.zeros((tq, 1), I32))

        def write(kt, carry):
            off = pl.multiple_of(kt * tk, tk)
            keys = key_scr[:, pl.ds(off, tk)]
            tie_ok = (keys == thr) & ((off + col0) <= cut)
            sel = (keys >= thr_eff) & ((keys > thr) | tie_ok | jnp.logical_not(finite))
            bias_ref[0, :, pl.ds(off, tk)] = jnp.where(sel, 0.0, NEG_BIAS).astype(bias_ref.dtype)
            return carry
        lax.fori_loop(0, nkt, write, 0)


def _dsa_select(iq, ikt, iw, ksel):
    b, _, s, _ = iq.shape
    tq, tk = min(SEL_TQ, s), min(SEL_TK, s)
    assert s % tk == 0 and tk % tq == 0 and s // LANES < 2 ** 15
    return pl.pallas_call(
        functools.partial(_select_kernel, tq=tq, tk=tk, ksel=ksel, seq=s),
        out_shape=jax.ShapeDtypeStruct((b, s, s), BF16),
        grid=(b, s // tq),
        in_specs=[pl.BlockSpec((1, IDX_HEADS, tq, IDX_HD), lambda bi, i: (bi, 0, i, 0)),
                  pl.BlockSpec((1, IDX_HD, s), lambda bi, i: (bi, 0, 0)),
                  pl.BlockSpec((1, tq, LANES), lambda bi, i: (bi, i, 0))],
        out_specs=pl.BlockSpec((1, tq, s), lambda bi, i: (bi, i, 0)),
        scratch_shapes=[pltpu.VMEM((tq, s), I32), pltpu.VMEM((tq, s), I16),
                        pltpu.VMEM((tq, s), I16), pltpu.VMEM((IDX_HEADS, tq, LANES), F32)],
        compiler_params=_cparams(("parallel", "parallel")), name="dsa_select",
    )(iq, ikt, iw)


def _attn_kernel(it_ref, jt_ref, q_ref, kt_ref, v_ref, b_ref, o_ref, *state, rows):
    m_scr = state[:SA_HEADS]
    acc_scr = state[SA_HEADS:]
    p = pl.program_id(1)
    i = it_ref[p]
    j = jt_ref[p]
    t = q_ref.shape[1]
    tk = kt_ref.shape[3]

    @pl.when(j == 0)
    def _():
        for h in range(SA_HEADS):
            m_scr[h][...] = jnp.full(m_scr[h].shape, NEG_BIAS, F32)
            acc_scr[h][...] = jnp.zeros_like(acc_scr[h])

    ones = jnp.ones((tk, SA_HD), BF16)
    eye = (lax.broadcasted_iota(I32, (rows, rows), 0)
           == lax.broadcasted_iota(I32, (rows, rows), 1)).astype(BF16)

    def update(r, carry):
        rs = pl.ds(pl.multiple_of(r * rows, rows), rows)
        heads = range(SA_HEADS)
        hs = [slice(h * SA_HD, (h + 1) * SA_HD) for h in heads]
        mask = b_ref[0, rs, :]
        s = [_dot(jnp.concatenate([q_ref[0, rs, hs[h]], eye], axis=1),
                  jnp.concatenate([kt_ref[0, h], mask], axis=0)) for h in heads]
        m_old = [m_scr[h][rs, :] for h in heads]
        m_new = [jnp.maximum(m_old[h], jnp.max(s[h], axis=1, keepdims=True)) for h in heads]
        pr = [jnp.exp2(s[h] - jnp.concatenate([m_new[h]] * (tk // LANES), axis=1)).astype(BF16)
              for h in heads]
        pv = [_dot(pr[h], jnp.concatenate([v_ref[0, :, hs[h]], ones], axis=1)) for h in heads]
        for h in heads:
            alpha = jnp.exp2(m_old[h] - m_new[h])
            acc_scr[h][rs, :] = acc_scr[h][rs, :] * jnp.concatenate([alpha, alpha], axis=1) + pv[h]
            m_scr[h][rs, :] = m_new[h]
        return carry

    lax.fori_loop(0, t // rows, update, 0)

    @pl.when(j == i)
    def _():
        for h in range(SA_HEADS):
            hs = slice(h * SA_HD, (h + 1) * SA_HD)
            o_ref[0, :, hs] = (acc_scr[h][:, :SA_HD] / acc_scr[h][:, SA_HD:]).astype(o_ref.dtype)


def _dsa_attend(q, k, v, bias):
    b, s, w = q.shape
    t = min(ATT_T, s)
    n = s // t
    pairs = [(i, j) for i in range(n) for j in range(i + 1)]
    it = jnp.asarray([p[0] for p in pairs], I32)
    jt = jnp.asarray([p[1] for p in pairs], I32)
    grid_spec = pltpu.PrefetchScalarGridSpec(
        num_scalar_prefetch=2, grid=(b, len(pairs)),
        in_specs=[pl.BlockSpec((1, t, w), lambda bi, p, it, jt: (bi, it[p], 0)),
                  pl.BlockSpec((1, SA_HEADS, SA_HD, t), lambda bi, p, it, jt: (bi, 0, 0, jt[p])),
                  pl.BlockSpec((1, t, w), lambda bi, p, it, jt: (bi, jt[p], 0)),
                  pl.BlockSpec((1, t, t), lambda bi, p, it, jt: (bi, it[p], jt[p]))],
        out_specs=pl.BlockSpec((1, t, w), lambda bi, p, it, jt: (bi, it[p], 0)),
        scratch_shapes=([pltpu.VMEM((t, LANES), F32) for _ in range(SA_HEADS)]
                        + [pltpu.VMEM((t, 2 * SA_HD), F32) for _ in range(SA_HEADS)]))
    return pl.pallas_call(
        functools.partial(_attn_kernel, rows=min(ATT_ROWS, t)),
        out_shape=jax.ShapeDtypeStruct((b, s, w), BF16), grid_spec=grid_spec,
        compiler_params=_cparams(("parallel", "arbitrary")), name="dsa_attend",
    )(it, jt, q, k, v, bias)


def _xattn_kernel(q_ref, mk_ref, mv_ref, o_ref):
    for h in range(XA_HEADS):
        hs = slice(h * XA_HD, (h + 1) * XA_HD)
        s = _dot_nt(q_ref[0, :, hs], mk_ref[0, :, hs]) * XA_HD ** -0.5
        p = jnp.exp(s - jnp.max(s, axis=1, keepdims=True))
        o = _dot(p.astype(BF16), mv_ref[0, :, hs]) / jnp.sum(p, axis=1, keepdims=True)
        o_ref[0, :, hs] = o.astype(o_ref.dtype)


def _mem_attend(xq, mk, mv, t):
    b, s, w = xq.shape
    t = min(t, s)
    m = mk.shape[1]
    return pl.pallas_call(
        _xattn_kernel,
        out_shape=jax.ShapeDtypeStruct((b, s, w), BF16),
        grid=(b, s // t),
        in_specs=[pl.BlockSpec((1, t, w), lambda bi, i: (bi, i, 0)),
                  pl.BlockSpec((1, m, w), lambda bi, i: (bi, 0, 0)),
                  pl.BlockSpec((1, m, w), lambda bi, i: (bi, 0, 0))],
        out_specs=pl.BlockSpec((1, t, w), lambda bi, i: (bi, i, 0)),
        compiler_params=_cparams(("parallel", "parallel")), name="mem_attend",
    )(xq, mk, mv)


def _merge_kernel(x_ref, od_ref, os_ref, ox_ref, gl_ref, bg_ref, wb_ref, wo_ref, o_ref):
    merged = jnp.zeros(x_ref.shape, F32)
    for r, br_ref in enumerate((od_ref, os_ref, ox_ref)):
        cs = slice(r * D_MODEL, (r + 1) * D_MODEL)
        gate = _sigmoid(gl_ref[:, cs] + bg_ref[:, cs])
        merged = merged + gate * _dot(br_ref[...], wb_ref[r])
    o_ref[...] = x_ref[...] + _dot(merged.astype(BF16), wo_ref[...])


def _merge(x2d, o_dn, o_sa, o_xa, gl, b_gate, w_branch, w_out, tm):
    rows, d = x2d.shape
    tm = min(tm, rows)
    row = pl.BlockSpec((tm, d), lambda i: (i, 0))
    return pl.pallas_call(
        _merge_kernel,
        out_shape=jax.ShapeDtypeStruct((rows, d), F32),
        grid=(rows // tm,),
        in_specs=[row, row, row, row,
                  pl.BlockSpec((tm, N_BRANCH * d), lambda i: (i, 0)),
                  pl.BlockSpec((1, N_BRANCH * d), lambda i: (0, 0)),
                  pl.BlockSpec((N_BRANCH, d, d), lambda i: (0, 0, 0)),
                  pl.BlockSpec((d, d), lambda i: (0, 0))],
        out_specs=row,
        compiler_params=_cparams(("parallel",)), name="merge",
    )(x2d, o_dn, o_sa, o_xa, gl, b_gate.reshape(1, N_BRANCH * d), w_branch, w_out)


def _ffn_kernel(x_ref, g_ref, wg_ref, wu_ref, wo_ref, o_ref, h_scr, acc_scr):
    f = pl.program_id(1)

    @pl.when(f == 0)
    def _():
        h_scr[...] = _rms(x_ref[...], g_ref[...]).astype(BF16)
        acc_scr[...] = x_ref[...]

    h = h_scr[...]
    gt = _dot(h, wg_ref[...])
    up = _dot(h, wu_ref[...])
    act = (gt * _sigmoid(gt) * up).astype(BF16)
    acc_scr[...] += _dot(act, wo_ref[...])

    @pl.when(f == pl.num_programs(1) - 1)
    def _():
        o_ref[...] = acc_scr[...]


def _ffn(x2d, g, wg, wu, wo, tm, tf):
    rows, d = x2d.shape
    tm = min(tm, rows)
    nf = wg.shape[1] // tf
    return pl.pallas_call(
        _ffn_kernel,
        out_shape=jax.ShapeDtypeStruct((rows, d), F32),
        grid=(rows // tm, nf),
        in_specs=[pl.BlockSpec((tm, d), lambda i, f: (i, 0)),
                  pl.BlockSpec((1, d), lambda i, f: (0, 0)),
                  pl.BlockSpec((d, tf), lambda i, f: (0, f)),
                  pl.BlockSpec((d, tf), lambda i, f: (0, f)),
                  pl.BlockSpec((tf, d), lambda i, f: (f, 0))],
        out_specs=pl.BlockSpec((tm, d), lambda i, f: (i, 0)),
        scratch_shapes=[pltpu.VMEM((tm, d), BF16), pltpu.VMEM((tm, d), F32)],
        compiler_params=_cparams(("parallel", "arbitrary")), name="ffn",
    )(x2d, g.reshape(1, d), wg, wu, wo)


def _final_norm_kernel(x_ref, g_ref, o_ref):
    o_ref[...] = _rms(x_ref[...], g_ref[...])


def _final_norm(x2d, g, tm):
    rows, d = x2d.shape
    tm = min(tm, rows)
    row = pl.BlockSpec((tm, d), lambda i: (i, 0))
    return pl.pallas_call(
        _final_norm_kernel, out_shape=jax.ShapeDtypeStruct((rows, d), F32), grid=(rows // tm,),
        in_specs=[row, pl.BlockSpec((1, d), lambda i: (0, 0))], out_specs=row,
        compiler_params=_cparams(("parallel",)), name="final_norm",
    )(x2d, g.reshape(1, d))


def _pad_cols(w, width):
    return jnp.pad(w, ((0, 0), (0, width - w.shape[1])))


def _layer(x2d, b, s, mem2d, tabs_sa, tabs_ix, norm_mix, norm_mem, w_in, b_gate, conv_w, a_log,
           dt_bias, dn_norm, w_mem_kv, w_branch, w_out, norm_ffn, w_ffn_in, w_ffn_out):
    wb = w_in.astype(BF16)
    w_small = _pad_cols(jnp.concatenate(
        [wb[:, OFF_DB:OFF_DB + DN_HEADS], wb[:, OFF_DA:OFF_DA + DN_HEADS],
         wb[:, OFF_IW:OFF_IW + IDX_HEADS]], axis=1), LANES)
    w_idx = _pad_cols(wb[:, OFF_IQ:OFF_IW], (IDX_HEADS + 2) * IDX_HD)

    dqkv, dz, small = _norm_matmul(
        x2d, norm_mix, [wb[:, OFF_DQKV:OFF_DZ], wb[:, OFF_DZ:OFF_DB], w_small],
        [F32, F32, F32], 256, "in_proj_dn")
    sqk, sv, idx = _norm_matmul(
        x2d, norm_mix, [wb[:, OFF_SQK:OFF_SV], wb[:, OFF_SV:OFF_IQ], w_idx],
        [F32, BF16, F32], 256, "in_proj_sa")
    xq, gl = _norm_matmul(
        x2d, norm_mix, [wb[:, OFF_XQ:OFF_GL], wb[:, OFF_GL:N_IN]], [BF16, F32], 256, "in_proj_xg")

    def r3(a):
        return a.reshape(b, s, a.shape[-1])

    dq, dk, dv, gb = _dn_prep(r3(dqkv), r3(small), conv_w, a_log, dt_bias, 256)
    o_dn = _dn_main(dq, dk, dv, r3(dz), gb, dn_norm)

    sq, sk, iq, ik, iw = _sa_prep(r3(sqk), r3(idx), r3(small), tabs_sa, tabs_ix, 256)
    ikt = jnp.swapaxes(ik[:, :, :IDX_HD], 1, 2)
    bias = _dsa_select(iq, ikt, iw, min(TOPK_MAX, s // 4))
    skt = jnp.transpose(sk.reshape(b, s, SA_HEADS, SA_HD), (0, 2, 3, 1))
    o_sa = _dsa_attend(sq, skt, r3(sv), bias)

    wkv = w_mem_kv.astype(BF16)
    mk, mv = _norm_matmul(mem2d, norm_mem, [wkv[:, :XA_WIDTH], wkv[:, XA_WIDTH:]],
                          [BF16, BF16], 256, "mem_kv")
    n_mem = mem2d.shape[0] // b
    o_xa = _mem_attend(r3(xq), mk.reshape(b, n_mem, XA_WIDTH), mv.reshape(b, n_mem, XA_WIDTH), 512)

    x2d = _merge(x2d, o_dn.reshape(b * s, -1), o_sa.reshape(b * s, -1), o_xa.reshape(b * s, -1),
                 gl, b_gate, w_branch.astype(BF16), w_out.astype(BF16), 256)
    wf = w_ffn_in.astype(BF16)
    return _ffn(x2d, norm_ffn, wf[:, :D_FF], wf[:, D_FF:], w_ffn_out.astype(BF16), 512, D_FF // 2)


def kernel(x, mem, positions, norm_mix, norm_mem, w_in, b_gate, conv_w, a_log, dt_bias, dn_norm,
           w_mem_kv, w_branch, w_out, norm_ffn, w_ffn_in, w_ffn_out, norm_final):
    b, s, d = x.shape
    assert d == D_MODEL and s % ATT_T == 0
    x2d = x.reshape(b * s, d)
    mem2d = mem.reshape(-1, d)
    tabs_sa = _rope_tables(positions, SA_HD, 512)
    tabs_ix = _rope_tables(positions, IDX_HD, 512)
    for l in range(w_in.shape[0]):
        x2d = _layer(x2d, b, s, mem2d, tabs_sa, tabs_ix, norm_mix[l], norm_mem[l], w_in[l],
                     b_gate[l], conv_w[l], a_log[l], dt_bias[l], dn_norm[l], w_mem_kv[l],
                     w_branch[l], w_out[l], norm_ffn[l], w_ffn_in[l], w_ffn_out[l])
    return _final_norm(x2d, norm_final, 512).reshape(b, s, d)
```
